```python
import math
import jax, jax.numpy as jnp
from jax import lax
import numpy as np

D_MODEL = 2048
BATCH = 4
SEQ = 4096
DEPTH = 2

N_MIXERS = 2
D_FF = 4 * D_MODEL
CONV_WIDTH = 3
N_HEADS = 16
HEAD_DIM_V = D_MODEL // N_HEADS
HEAD_DIM_QK = HEAD_DIM_V // 2
QBLK = 128
ATTN_SCALE = HEAD_DIM_QK ** -0.5
NORM_EPS = 1e-6
SUBLN_EPS = 1e-5
N_CONV_LAYERS = (DEPTH + 1) // 2
N_ATTN_LAYERS = DEPTH // 2

kernel_name = "hybrid_shortconv_diffattn_encoder"


def _rmsnorm(x, g, eps=NORM_EPS):
    xf = x.astype(jnp.float32)
    y = xf * lax.rsqrt(jnp.mean(xf * xf, axis=-1, keepdims=True) + eps)
    return (y * g.astype(jnp.float32)).astype(x.dtype)


def _alibi_slopes():
    h = jnp.arange(1, N_HEADS + 1, dtype=jnp.float32)
    return jnp.exp2(-8.0 * h / N_HEADS)


def _lambda_init(layer_idx):
    return 0.8 - 0.6 * math.exp(-0.3 * layer_idx)


def _short_conv_mixer(h, w_in, w_conv, w_out):
    bcu = h @ w_in
    gate_b, gate_c, u = jnp.split(bcu, 3, axis=-1)
    u = gate_c * u
    up = jnp.pad(u, ((0, 0), (1, 1), (0, 0)))
    conv = w_conv[0] * up[:, :-2] + w_conv[1] * up[:, 1:-1] + w_conv[2] * up[:, 2:]
    return (gate_b * conv) @ w_out


def _diff_attention(h, w_qkv, lq1, lk1, lq2, lk2, subln, w_o, lambda_init):
    bsz, seq, _ = h.shape
    qkv = h @ w_qkv
    q, k, v = jnp.split(qkv, 3, axis=-1)
    q = q.reshape(bsz, seq, N_HEADS, 2, HEAD_DIM_QK)
    k = k.reshape(bsz, seq, N_HEADS, 2, HEAD_DIM_QK)
    v = v.reshape(bsz, seq, N_HEADS, HEAD_DIM_V)
    f32 = jnp.float32
    lam = (jnp.exp(jnp.sum(lq1.astype(f32) * lk1.astype(f32)))
           - jnp.exp(jnp.sum(lq2.astype(f32) * lk2.astype(f32))) + lambda_init)
    slopes = _alibi_slopes()
    kpos = jnp.arange(seq, dtype=jnp.int32)
    n_blk = seq // QBLK
    q_blocks = q.reshape(bsz, n_blk, QBLK, N_HEADS, 2, HEAD_DIM_QK).transpose(1, 0, 2, 3, 4, 5)
    q_pos = kpos.reshape(n_blk, QBLK)

    def block(args):
        qb, pos = args
        s = jnp.einsum('bqhcd,bkhcd->bchqk', qb, k).astype(f32) * ATTN_SCALE
        dist = jnp.abs(pos[:, None] - kpos[None, :]).astype(f32)
        bias = -slopes[:, None, None] * dist
        p = jax.nn.softmax(s + bias, axis=-1)
        a = p[:, 0] - lam * p[:, 1]
        return jnp.einsum('bhqk,bkhd->bqhd', a.astype(v.dtype), v)

    o = lax.map(block, (q_blocks, q_pos))
    o = o.transpose(1, 0, 2, 3, 4).reshape(bsz, seq, N_HEADS, HEAD_DIM_V)
    o = _rmsnorm(o, subln, SUBLN_EPS) * (1.0 - lambda_init)
    return o.reshape(bsz, seq, D_MODEL) @ w_o


def _sqrelu_mlp(h, w1, w2):
    a = jax.nn.relu(h @ w1)
    return (a * a) @ w2


def setup_inputs(seed: int = 0) -> dict:
    key = jax.random.key(seed)
    ks = jax.random.split(key, 20)
    f32 = jnp.float32
    D, F = D_MODEL, D_FF
    nrm = lambda k, shape, scale: jax.random.normal(k, shape, f32) * scale
    return {
        "x": nrm(ks[0], (BATCH, SEQ, D), 1.0),
        "ln_mix": 1.0 + nrm(ks[1], (DEPTH, D), 0.02),
        "ln_mlp": 1.0 + nrm(ks[2], (DEPTH, D), 0.02),
        "conv_w_in": nrm(ks[3], (N_CONV_LAYERS, D, 3 * D), D ** -0.5),
        "conv_w": nrm(ks[4], (N_CONV_LAYERS, CONV_WIDTH, D), CONV_WIDTH ** -0.5),
        "conv_w_out": nrm(ks[5], (N_CONV_LAYERS, D, D), D ** -0.5),
        "attn_w_qkv": nrm(ks[6], (N_ATTN_LAYERS, D, 3 * D), D ** -0.5),
        "attn_lambda_q1": nrm(ks[7], (N_ATTN_LAYERS, HEAD_DIM_QK), 0.1),
        "attn_lambda_k1": nrm(ks[8], (N_ATTN_LAYERS, HEAD_DIM_QK), 0.1),
        "attn_lambda_q2": nrm(ks[9], (N_ATTN_LAYERS, HEAD_DIM_QK), 0.1),
        "attn_lambda_k2": nrm(ks[10], (N_ATTN_LAYERS, HEAD_DIM_QK), 0.1),
        "attn_subln": 1.0 + nrm(ks[11], (N_ATTN_LAYERS, HEAD_DIM_V), 0.02),
        "attn_w_o": nrm(ks[12], (N_ATTN_LAYERS, D, D), D ** -0.5),
        "mlp_w1": nrm(ks[13], (DEPTH, D, F), D ** -0.5),
        "mlp_w2": nrm(ks[14], (DEPTH, F, D), F ** -0.5),
        "ln_f": 1.0 + nrm(ks[15], (D,), 0.02),
    }


def reference(x, ln_mix, ln_mlp, conv_w_in, conv_w, conv_w_out, attn_w_qkv,
              attn_lambda_q1, attn_lambda_k1, attn_lambda_q2, attn_lambda_k2,
              attn_subln, attn_w_o, mlp_w1, mlp_w2, ln_f):
    h = x
    for i in range(DEPTH):
        j = i // N_MIXERS
        hn = _rmsnorm(h, ln_mix[i])
        if i % N_MIXERS == 0:
            mix = _short_conv_mixer(hn, conv_w_in[j], conv_w[j], conv_w_out[j])
        else:
            mix = _diff_attention(hn, attn_w_qkv[j], attn_lambda_q1[j], attn_lambda_k1[j],
                                  attn_lambda_q2[j], attn_lambda_k2[j], attn_subln[j],
                                  attn_w_o[j], _lambda_init(i))
        h = h + mix
        h = h + _sqrelu_mlp(_rmsnorm(h, ln_mlp[i]), mlp_w1[i], mlp_w2[i])
    return _rmsnorm(h, ln_f)
```

```python
import functools
import math

import jax
import jax.numpy as jnp
from jax import lax
from jax.experimental import pallas as pl
from jax.experimental.pallas import tpu as pltpu

N_HEADS = 16
HEAD_DIM = 128
HALF_DIM = HEAD_DIM // 2
ATTN_SCALE = HALF_DIM ** -0.5
NORM_EPS = 1e-6
SUBLN_EPS = 1e-5
N_MIXERS = 2

V7X_VMEM_BYTES = 64 * 1024 * 1024
VMEM_LIMIT_BYTES = V7X_VMEM_BYTES - 8 * 1024 * 1024
BF16_SUBLANES = 16
NORM_ROWS = 256

F32 = jnp.float32
BF16 = jnp.bfloat16


def _lambda_init(layer_idx):
    return 0.8 - 0.6 * math.exp(-0.3 * layer_idx)


def _rmsnorm(x, g, eps):
    ms = jnp.mean(x * x, axis=-1, keepdims=True)
    return x * lax.rsqrt(ms + eps) * g


def _norm_rows_to(dst_ref, src_ref, g_ref, eps):
    rows = src_ref.shape[0]
    g = g_ref[...]

    def body(c, carry):
        sl = pl.ds(pl.multiple_of(c * NORM_ROWS, NORM_ROWS), NORM_ROWS)
        dst_ref[sl, :] = _rmsnorm(src_ref[sl, :], g, eps).astype(dst_ref.dtype)
        return carry

    lax.fori_loop(0, rows // NORM_ROWS, body, 0)


def _params(n_axes):
    return pltpu.CompilerParams(
        dimension_semantics=("arbitrary",) * n_axes,
        vmem_limit_bytes=VMEM_LIMIT_BYTES)


def _norm_matmul_kernel(x_ref, g_ref, w_ref, o_ref, hn_ref):
    @pl.when(pl.program_id(1) == 0)
    def _():
        _norm_rows_to(hn_ref, x_ref, g_ref, NORM_EPS)

    o_ref[...] = jnp.dot(hn_ref[...], w_ref[...],
                         preferred_element_type=F32).astype(o_ref.dtype)


def _norm_matmul(x, g, w, *, tm=1024, tn=512):
    n, d = x.shape
    n_out = w.shape[1]
    return pl.pallas_call(
        _norm_matmul_kernel,
        grid=(n // tm, n_out // tn),
        in_specs=[
            pl.BlockSpec((tm, d), lambda i, j: (i, 0)),
            pl.BlockSpec((1, d), lambda i, j: (0, 0)),
            pl.BlockSpec((d, tn), lambda i, j: (0, j)),
        ],
        out_specs=pl.BlockSpec((tm, tn), lambda i, j: (i, j)),
        out_shape=jax.ShapeDtypeStruct((n, n_out), BF16),
        scratch_shapes=[pltpu.VMEM((tm, d), BF16)],
        compiler_params=_params(2),
        name="norm_matmul",
    )(x, g.reshape(1, d), w)


def _conv_mix_kernel(b_ref, c_ref, u_ref, cp_ref, up_ref, cn_ref, un_ref,
                     x_ref, cw_ref, wo_ref, o_ref, *, seq):
    i = pl.program_id(0)
    tm = b_ref.shape[0]
    cu = c_ref[...].astype(F32) * u_ref[...].astype(F32)
    last = BF16_SUBLANES - 1
    prev = cp_ref[last:, :].astype(F32) * up_ref[last:, :].astype(F32)
    nxt = cn_ref[:1, :].astype(F32) * un_ref[:1, :].astype(F32)
    prev = jnp.where((i * tm) % seq == 0, 0.0, prev)
    nxt = jnp.where(((i + 1) * tm) % seq == 0, 0.0, nxt)
    row = lax.broadcasted_iota(jnp.int32, cu.shape, 0)
    up = jnp.where(row == 0, prev, pltpu.roll(cu, 1, axis=0))
    dn = jnp.where(row == tm - 1, nxt, pltpu.roll(cu, tm - 1, axis=0))
    cw = cw_ref[...]
    conv = cw[0:1, :] * up + cw[1:2, :] * cu + cw[2:3, :] * dn
    g = (b_ref[...].astype(F32) * conv).astype(BF16)
    o_ref[...] = x_ref[...] + jnp.dot(g, wo_ref[...], preferred_element_type=F32)


def _conv_mix(bcu, x, conv_w, w_out, *, seq, tm=512):
    n, d = x.shape
    hb = BF16_SUBLANES
    tiles_per_halo = tm // hb
    n_halo = n // hb
    prev_idx = lambda i: jnp.maximum(i * tiles_per_halo - 1, 0)
    next_idx = lambda i: jnp.minimum((i + 1) * tiles_per_halo, n_halo - 1)
    return pl.pallas_call(
        functools.partial(_conv_mix_kernel, seq=seq),
        grid=(n // tm,),
        in_specs=[
            pl.BlockSpec((tm, d), lambda i: (i, 0)),
            pl.BlockSpec((tm, d), lambda i: (i, 1)),
            pl.BlockSpec((tm, d), lambda i: (i, 2)),
            pl.BlockSpec((hb, d), lambda i: (prev_idx(i), 1)),
            pl.BlockSpec((hb, d), lambda i: (prev_idx(i), 2)),
            pl.BlockSpec((hb, d), lambda i: (next_idx(i), 1)),
            pl.BlockSpec((hb, d), lambda i: (next_idx(i), 2)),
            pl.BlockSpec((tm, d), lambda i: (i, 0)),
            pl.BlockSpec((3, d), lambda i: (0, 0)),
            pl.BlockSpec((d, d), lambda i: (0, 0)),
        ],
        out_specs=pl.BlockSpec((tm, d), lambda i: (i, 0)),
        out_shape=jax.ShapeDtypeStruct((n, d), F32),
        compiler_params=_params(1),
        name="conv_mix",
    )(bcu, bcu, bcu, bcu, bcu, bcu, bcu, x, conv_w, w_out)


def _mlp_kernel(x_ref, g_ref, w1_ref, w2_ref, gf_ref, o_ref, hn_ref, *, final_norm):
    f = pl.program_id(1)

    @pl.when(f == 0)
    def _():
        _norm_rows_to(hn_ref, x_ref, g_ref, NORM_EPS)
        o_ref[...] = x_ref[...]

    a = jnp.dot(hn_ref[...], w1_ref[...], preferred_element_type=F32)
    a = jnp.maximum(a, 0.0)
    a = (a * a).astype(BF16)
    o_ref[...] += jnp.dot(a, w2_ref[...], preferred_element_type=F32)

    if final_norm:
        @pl.when(f == pl.num_programs(1) - 1)
        def _():
            _norm_rows_to(o_ref, o_ref, gf_ref, NORM_EPS)


def _mlp(x, g, w1, w2, g_final, *, tm=1024, tf=512):
    n, d = x.shape
    d_ff = w1.shape[1]
    final_norm = g_final is not None
    gf = (g_final if final_norm else g).reshape(1, d)
    return pl.pallas_call(
        functools.partial(_mlp_kernel, final_norm=final_norm),
        grid=(n // tm, d_ff // tf),
        in_specs=[
            pl.BlockSpec((tm, d), lambda i, f: (i, 0)),
            pl.BlockSpec((1, d), lambda i, f: (0, 0)),
            pl.BlockSpec((d, tf), lambda i, f: (0, f)),
            pl.BlockSpec((tf, d), lambda i, f: (f, 0)),
            pl.BlockSpec((1, d), lambda i, f: (0, 0)),
        ],
        out_specs=pl.BlockSpec((tm, d), lambda i, f: (i, 0)),
        out_shape=jax.ShapeDtypeStruct((n, d), F32),
        scratch_shapes=[pltpu.VMEM((tm, d), BF16)],
        compiler_params=_params(2),
        name="mlp",
    )(x, g.reshape(1, d), w1, w2, gf)


def _attn_kernel(lq1_ref, lk1_ref, lq2_ref, lk2_ref, subln_ref, q_ref, k_ref, v_ref,
                 o_ref, bias_ref, t0_ref, t1_ref, *, kc, lambda_init):
    h = pl.program_id(0)
    qi = pl.program_id(1)
    b = pl.program_id(2)
    tq = q_ref.shape[0]
    seq = k_ref.shape[0]
    n_kc = seq // kc

    def chunk(j):
        return pl.ds(pl.multiple_of(j * kc, kc), kc)

    @pl.when(b == 0)
    def _():
        neg_slope = -jnp.exp2(jnp.full((1, 1), -8.0 / N_HEADS, F32) * (h + 1).astype(F32))

        def body(j, carry):
            rows = qi * tq + lax.broadcasted_iota(jnp.int32, (tq, kc), 0)
            cols = j * kc + lax.broadcasted_iota(jnp.int32, (tq, kc), 1)
            bias_ref[:, chunk(j)] = neg_slope * jnp.abs(rows - cols).astype(F32)
            return carry

        lax.fori_loop(0, n_kc, body, 0)

    q = q_ref[...] * jnp.asarray(ATTN_SCALE, BF16)
    lane = lax.broadcasted_iota(jnp.int32, q.shape, 1)
    q0 = jnp.where(lane < HALF_DIM, q, jnp.zeros_like(q))
    q1 = jnp.where(lane >= HALF_DIM, q, jnp.zeros_like(q))
    nt = (((1,), (1,)), ((), ()))

    def scores(j, carry):
        m0, m1 = carry
        kj = k_ref[chunk(j), :]
        bj = bias_ref[:, chunk(j)]
        s0 = lax.dot_general(q0, kj, nt, preferred_element_type=F32) + bj
        s1 = lax.dot_general(q1, kj, nt, preferred_element_type=F32) + bj
        t0_ref[:, chunk(j)] = s0
        t1_ref[:, chunk(j)] = s1
        m0 = jnp.maximum(m0, jnp.max(s0, axis=-1, keepdims=True))
        m1 = jnp.maximum(m1, jnp.max(s1, axis=-1, keepdims=True))
        return m0, m1

    neg_inf = jnp.full((tq, 1), -jnp.inf, F32)
    m0, m1 = lax.fori_loop(0, n_kc, scores, (neg_inf, neg_inf))

    def weighted(j, carry):
        l0, l1, a0, a1 = carry
        e0 = jnp.exp(t0_ref[:, chunk(j)] - m0)
        e1 = jnp.exp(t1_ref[:, chunk(j)] - m1)
        vj = v_ref[chunk(j), :]
        l0 = l0 + jnp.sum(e0, axis=-1, keepdims=True)
        l1 = l1 + jnp.sum(e1, axis=-1, keepdims=True)
        a0 = a0 + jnp.dot(e0.astype(BF16), vj, preferred_element_type=F32)
        a1 = a1 + jnp.dot(e1.astype(BF16), vj, preferred_element_type=F32)
        return l0, l1, a0, a1

    zc = jnp.zeros((tq, 1), F32)
    za = jnp.zeros((tq, HEAD_DIM), F32)
    l0, l1, a0, a1 = lax.fori_loop(0, n_kc, weighted, (zc, zc, za, za))

    lam = (jnp.exp(jnp.sum(lq1_ref[...] * lk1_ref[...], keepdims=True))
           - jnp.exp(jnp.sum(lq2_ref[...] * lk2_ref[...], keepdims=True)) + lambda_init)
    o = a0 / l0 - lam * (a1 / l1)
    o = _rmsnorm(o, subln_ref[...], SUBLN_EPS) * (1.0 - lambda_init)
    o_ref[...] = o.astype(o_ref.dtype)


def _attention(qkv, lq1, lk1, lq2, lk2, subln, *, batch, seq, lambda_init, tq=512, kc=512):
    n = qkv.shape[0]
    d = N_HEADS * HEAD_DIM
    q_tiles = seq // tq
    vec = lambda a: a.reshape(1, -1).astype(F32)
    small = lambda w: pl.BlockSpec((1, w), lambda h, qi, b: (0, 0))
    return pl.pallas_call(
        functools.partial(_attn_kernel, kc=kc, lambda_init=lambda_init),
        grid=(N_HEADS, q_tiles, batch),
        in_specs=[
            small(HALF_DIM), small(HALF_DIM), small(HALF_DIM), small(HALF_DIM),
            small(HEAD_DIM),
            pl.BlockSpec((tq, HEAD_DIM), lambda h, qi, b: (b * q_tiles + qi, h)),
            pl.BlockSpec((seq, HEAD_DIM), lambda h, qi, b: (b, N_HEADS + h)),
            pl.BlockSpec((seq, HEAD_DIM), lambda h, qi, b: (b, 2 * N_HEADS + h)),
        ],
        out_specs=pl.BlockSpec((tq, HEAD_DIM), lambda h, qi, b: (b * q_tiles + qi, h)),
        out_shape=jax.ShapeDtypeStruct((n, d), BF16),
        scratch_shapes=[
            pltpu.VMEM((tq, seq), F32),
            pltpu.VMEM((tq, seq), F32),
            pltpu.VMEM((tq, seq), F32),
        ],
        compiler_params=_params(3),
        name="diff_attention",
    )(vec(lq1), vec(lk1), vec(lq2), vec(lk2), vec(subln), qkv, qkv, qkv)


def _proj_residual_kernel(a_ref, x_ref, w_ref, o_ref):
    o_ref[...] = x_ref[...] + jnp.dot(a_ref[...], w_ref[...], preferred_element_type=F32)


def _proj_residual(a, x, w, *, tm=512):
    n, d = x.shape
    return pl.pallas_call(
        _proj_residual_kernel,
        grid=(n // tm,),
        in_specs=[
            pl.BlockSpec((tm, d), lambda i: (i, 0)),
            pl.BlockSpec((tm, d), lambda i: (i, 0)),
            pl.BlockSpec((d, d), lambda i: (0, 0)),
        ],
        out_specs=pl.BlockSpec((tm, d), lambda i: (i, 0)),
        out_shape=jax.ShapeDtypeStruct((n, d), F32),
        compiler_params=_params(1),
        name="proj_residual",
    )(a, x, w)


def kernel(x, ln_mix, ln_mlp, conv_w_in, conv_w, conv_w_out, attn_w_qkv, attn_lambda_q1,
           attn_lambda_k1, attn_lambda_q2, attn_lambda_k2, attn_subln, attn_w_o, mlp_w1,
           mlp_w2, ln_f):
    batch, seq, d = x.shape
    depth = ln_mix.shape[0]
    h = x.reshape(batch * seq, d)
    for i in range(depth):
        j = i // N_MIXERS
        if i % N_MIXERS == 0:
            bcu = _norm_matmul(h, ln_mix[i], conv_w_in[j].astype(BF16))
            h = _conv_mix(bcu, h, conv_w[j], conv_w_out[j].astype(BF16), seq=seq)
        else:
            qkv = _norm_matmul(h, ln_mix[i], attn_w_qkv[j].astype(BF16))
            o = _attention(qkv, attn_lambda_q1[j], attn_lambda_k1[j], attn_lambda_q2[j],
                           attn_lambda_k2[j], attn_subln[j], batch=batch, seq=seq,
                           lambda_init=_lambda_init(i))
            h = _proj_residual(o, h, attn_w_o[j].astype(BF16))
        g_final = ln_f if i == depth - 1 else None
        h = _mlp(h, ln_mlp[i], mlp_w1[i].astype(BF16), mlp_w2[i].astype(BF16), g_final)
    return h.reshape(batch, seq, d)
```

```python
import functools
import math

import jax
import jax.numpy as jnp
from jax import lax
from jax.experimental import pallas as pl
from jax.experimental.pallas import tpu as pltpu

N_HEADS = 16
HEAD_DIM = 128
HALF_DIM = HEAD_DIM // 2
ATTN_SCALE = HALF_DIM ** -0.5
LOG2E = math.log2(math.e)
NORM_EPS = 1e-6
SUBLN_EPS = 1e-5
N_MIXERS = 2

V7X_VMEM_BYTES = 64 * 1024 * 1024
VMEM_LIMIT_BYTES = V7X_VMEM_BYTES - 8 * 1024 * 1024
SUBLANES = 8
BF16_SUBLANES = 16
NORM_ROWS = 256
ATTN_KEY_CHUNK = 512

F32 = jnp.float32
BF16 = jnp.bfloat16
NT_DIMS = (((1,), (1,)), ((), ()))


def _lambda_init(layer_idx):
    return 0.8 - 0.6 * math.exp(-0.3 * layer_idx)


def _rmsnorm(x, g, eps):
    ms = jnp.mean(x * x, axis=-1, keepdims=True)
    return x * lax.rsqrt(ms + eps) * g


def _norm_rows_to(dst_ref, src_ref, g_ref, eps):
    rows = src_ref.shape[0]
    g = g_ref[...]

    def body(c, carry):
        sl = pl.ds(pl.multiple_of(c * NORM_ROWS, NORM_ROWS), NORM_ROWS)
        dst_ref[sl, :] = _rmsnorm(src_ref[sl, :], g, eps).astype(dst_ref.dtype)
        return carry

    lax.fori_loop(0, rows // NORM_ROWS, body, 0)


def _params(n_axes):
    return pltpu.CompilerParams(
        dimension_semantics=("arbitrary",) * n_axes,
        vmem_limit_bytes=VMEM_LIMIT_BYTES)


def _norm_matmul_kernel(x_ref, g_ref, w_ref, o_ref, hn_ref, *, scaled_tiles, scale):
    j = pl.program_id(1)

    @pl.when(j == 0)
    def _():
        _norm_rows_to(hn_ref, x_ref, g_ref, NORM_EPS)

    acc = jnp.dot(hn_ref[...], w_ref[...], preferred_element_type=F32)
    if scaled_tiles:
        acc = acc * jnp.where(j < scaled_tiles, scale, 1.0).astype(F32)
    o_ref[...] = acc.astype(o_ref.dtype)


def _norm_matmul(x, g, w, *, scaled_cols=0, scale=1.0, tm=1024, tn=512):
    n, d = x.shape
    n_out = w.shape[1]
    return pl.pallas_call(
        functools.partial(_norm_matmul_kernel, scaled_tiles=scaled_cols // tn, scale=scale),
        grid=(n // tm, n_out // tn),
        in_specs=[
            pl.BlockSpec((tm, d), lambda i, j: (i, 0)),
            pl.BlockSpec((1, d), lambda i, j: (0, 0)),
            pl.BlockSpec((d, tn), lambda i, j: (0, j)),
        ],
        out_specs=[pl.BlockSpec((tm, tn), lambda i, j: (i, j)),
                   pl.BlockSpec((tm, d), lambda i, j: (i, 0))],
        out_shape=[jax.ShapeDtypeStruct((n, n_out), BF16),
                   jax.ShapeDtypeStruct((n, d), BF16)],
        compiler_params=_params(2),
        name="norm_matmul",
    )(x, g.reshape(1, d), w)


def _matmul_t_kernel(a_ref, wt_ref, o_ref):
    chunk = o_ref.shape[2]
    out = lax.dot_general(wt_ref[...], a_ref[...], NT_DIMS,
                          preferred_element_type=F32).astype(o_ref.dtype)
    for c in range(o_ref.shape[0]):
        o_ref[c] = out[:, c * chunk:(c + 1) * chunk]


def _matmul_t(a, wt, *, chunk, tm=1024, tn=1024):
    n, d = a.shape
    n_out = wt.shape[0]
    return pl.pallas_call(
        _matmul_t_kernel,
        grid=(n // tm, n_out // tn),
        in_specs=[
            pl.BlockSpec((tm, d), lambda i, j: (i, 0)),
            pl.BlockSpec((tn, d), lambda i, j: (j, 0)),
        ],
        out_specs=pl.BlockSpec((tm // chunk, tn, chunk), lambda i, j: (i, j, 0)),
        out_shape=jax.ShapeDtypeStruct((n // chunk, n_out, chunk), BF16),
        compiler_params=_params(2),
        name="matmul_t",
    )(a, wt)


def _conv_mix_kernel(b_ref, c_ref, u_ref, cp_ref, up_ref, cn_ref, un_ref,
                     x_ref, cw_ref, wo_ref, o_ref, *, seq):
    i = pl.program_id(0)
    tm = b_ref.shape[0]
    cu = c_ref[...].astype(F32) * u_ref[...].astype(F32)
    last = BF16_SUBLANES - 1
    prev = cp_ref[last:, :].astype(F32) * up_ref[last:, :].astype(F32)
    nxt = cn_ref[:1, :].astype(F32) * un_ref[:1, :].astype(F32)
    prev = jnp.where((i * tm) % seq == 0, 0.0, prev)
    nxt = jnp.where(((i + 1) * tm) % seq == 0, 0.0, nxt)
    row = lax.broadcasted_iota(jnp.int32, cu.shape, 0)
    up = jnp.where(row == 0, prev, pltpu.roll(cu, 1, axis=0))
    dn = jnp.where(row == tm - 1, nxt, pltpu.roll(cu, tm - 1, axis=0))
    cw = cw_ref[...]
    conv = cw[0:1, :] * up + cw[1:2, :] * cu + cw[2:3, :] * dn
    g = (b_ref[...].astype(F32) * conv).astype(BF16)
    o_ref[...] = x_ref[...] + jnp.dot(g, wo_ref[...], preferred_element_type=F32)


def _conv_mix(bcu, x, conv_w, w_out, *, seq, tm=512):
    n, d = x.shape
    hb = BF16_SUBLANES
    tiles_per_halo = tm // hb
    n_halo = n // hb
    prev_idx = lambda i: jnp.maximum(i * tiles_per_halo - 1, 0)
    next_idx = lambda i: jnp.minimum((i + 1) * tiles_per_halo, n_halo - 1)
    return pl.pallas_call(
        functools.partial(_conv_mix_kernel, seq=seq),
        grid=(n // tm,),
        in_specs=[
            pl.BlockSpec((tm, d), lambda i: (i, 0)),
            pl.BlockSpec((tm, d), lambda i: (i, 1)),
            pl.BlockSpec((tm, d), lambda i: (i, 2)),
            pl.BlockSpec((hb, d), lambda i: (prev_idx(i), 1)),
            pl.BlockSpec((hb, d), lambda i: (prev_idx(i), 2)),
            pl.BlockSpec((hb, d), lambda i: (next_idx(i), 1)),
            pl.BlockSpec((hb, d), lambda i: (next_idx(i), 2)),
            pl.BlockSpec((tm, d), lambda i: (i, 0)),
            pl.BlockSpec((3, d), lambda i: (0, 0)),
            pl.BlockSpec((d, d), lambda i: (0, 0)),
        ],
        out_specs=pl.BlockSpec((tm, d), lambda i: (i, 0)),
        out_shape=jax.ShapeDtypeStruct((n, d), F32),
        compiler_params=_params(1),
        name="conv_mix",
    )(bcu, bcu, bcu, bcu, bcu, bcu, bcu, x, conv_w, w_out)


def _mlp_kernel(x_ref, g_ref, w1_ref, w2_ref, gf_ref, o_ref, hn_ref, *, final_norm):
    f = pl.program_id(1)

    @pl.when(f == 0)
    def _():
        _norm_rows_to(hn_ref, x_ref, g_ref, NORM_EPS)
        o_ref[...] = x_ref[...]

    a = jnp.dot(hn_ref[...], w1_ref[...], preferred_element_type=F32)
    a = jnp.maximum(a, 0.0)
    a = (a * a).astype(BF16)
    o_ref[...] += jnp.dot(a, w2_ref[...], preferred_element_type=F32)

    if final_norm:
        @pl.when(f == pl.num_programs(1) - 1)
        def _():
            _norm_rows_to(o_ref, o_ref, gf_ref, NORM_EPS)


def _mlp(x, g, w1, w2, g_final, *, tm=1024, tf=512):
    n, d = x.shape
    d_ff = w1.shape[1]
    final_norm = g_final is not None
    gf = (g_final if final_norm else g).reshape(1, d)
    return pl.pallas_call(
        functools.partial(_mlp_kernel, final_norm=final_norm),
        grid=(n // tm, d_ff // tf),
        in_specs=[
            pl.BlockSpec((tm, d), lambda i, f: (i, 0)),
            pl.BlockSpec((1, d), lambda i, f: (0, 0)),
            pl.BlockSpec((d, tf), lambda i, f: (0, f)),
            pl.BlockSpec((tf, d), lambda i, f: (f, 0)),
            pl.BlockSpec((1, d), lambda i, f: (0, 0)),
        ],
        out_specs=pl.BlockSpec((tm, d), lambda i, f: (i, 0)),
        out_shape=jax.ShapeDtypeStruct((n, d), F32),
        scratch_shapes=[pltpu.VMEM((tm, d), BF16)],
        compiler_params=_params(2),
        name="mlp",
    )(x, g.reshape(1, d), w1, w2, gf)


def _attn_kernel(lq1_ref, lk1_ref, lq2_ref, lk2_ref, subln_ref, q_ref, k_ref, vt_ref,
                 o_ref, bias_ref, s_ref, *, lambda_init):
    h = pl.program_id(0)
    qi = pl.program_id(1)
    b = pl.program_id(2)
    tq = q_ref.shape[0]
    seq = k_ref.shape[0]
    n_kc, _, kc = vt_ref.shape
    w = 2 * tq

    def key_rows(j):
        return pl.ds(pl.multiple_of(j * kc, kc), kc)

    @pl.when(b == 0)
    def _():
        slope = jnp.exp2(jnp.full((1, 1), -8.0 / N_HEADS, F32) * (h + 1).astype(F32))
        neg_slope = -slope * LOG2E

        def body(j, carry):
            key = j * kc + lax.broadcasted_iota(jnp.int32, (kc, tq), 0)
            qry = qi * tq + lax.broadcasted_iota(jnp.int32, (kc, tq), 1)
            bias_ref[key_rows(j), :] = neg_slope * jnp.abs(qry - key).astype(F32)
            return carry

        lax.fori_loop(0, n_kc, body, 0)

    q = q_ref[...]
    lane = lax.broadcasted_iota(jnp.int32, q.shape, 1)
    zero = jnp.zeros_like(q)
    q_both = jnp.concatenate([jnp.where(lane < HALF_DIM, q, zero),
                              jnp.where(lane >= HALF_DIM, q, zero)], axis=0)

    ones_rows = (lax.broadcasted_iota(jnp.int32, (BF16_SUBLANES, kc), 0) == 0).astype(BF16)

    m_run = jnp.full((SUBLANES, w), -1e30, F32)
    for j in range(n_kc):
        s = lax.dot_general(k_ref[j * kc:(j + 1) * kc, :], q_both, NT_DIMS,
                            preferred_element_type=F32)
        for r in range(kc // SUBLANES):
            rows = slice(j * kc + r * SUBLANES, j * kc + (r + 1) * SUBLANES)
            br = bias_ref[rows, :]
            t = s[r * SUBLANES:(r + 1) * SUBLANES, :] + jnp.concatenate([br, br], axis=1)
            s_ref[rows, :] = t
            m_run = jnp.maximum(m_run, t)
    m = jnp.broadcast_to(jnp.max(m_run, axis=0, keepdims=True), (SUBLANES, w))

    acc = None
    slab = BF16_SUBLANES
    for j in range(n_kc):
        parts = []
        for r in range(kc // slab):
            rows = slice(j * kc + r * slab, j * kc + (r + 1) * slab)
            x = s_ref[rows, :].reshape(slab // SUBLANES, SUBLANES, w) - m[None]
            parts.append(jnp.exp2(x.reshape(slab, w).astype(BF16)))
        e = jnp.concatenate(parts, axis=0)
        v_aug = jnp.concatenate([vt_ref[j], ones_rows], axis=0)
        pv = jnp.dot(v_aug, e, preferred_element_type=F32)
        acc = pv if acc is None else acc + pv

    a0, a1 = acc[:HEAD_DIM, :tq], acc[:HEAD_DIM, tq:]
    l0, l1 = acc[HEAD_DIM:HEAD_DIM + 1, :tq], acc[HEAD_DIM:HEAD_DIM + 1, tq:]
    lam = (jnp.exp(jnp.sum(lq1_ref[...] * lk1_ref[...], keepdims=True))
           - jnp.exp(jnp.sum(lq2_ref[...] * lk2_ref[...], keepdims=True)) + lambda_init)
    o_t = a0 / l0 - lam * (a1 / l1)
    ms = jnp.mean(o_t * o_t, axis=0, keepdims=True)
    o_t = o_t * lax.rsqrt(ms + SUBLN_EPS) * subln_ref[...] * (1.0 - lambda_init)
    o_ref[...] = o_t.T.astype(o_ref.dtype)


def _attention(qk, vt, lq1, lk1, lq2, lk2, subln, *, batch, seq, lambda_init, tq=256):
    n = qk.shape[0]
    d = N_HEADS * HEAD_DIM
    kc = vt.shape[2]
    chunks = seq // kc
    q_tiles = seq // tq
    vec = lambda a: a.reshape(1, -1).astype(F32)
    small = lambda w: pl.BlockSpec((1, w), lambda h, qi, b: (0, 0))
    return pl.pallas_call(
        functools.partial(_attn_kernel, lambda_init=lambda_init),
        grid=(N_HEADS, q_tiles, batch),
        in_specs=[
            small(HALF_DIM), small(HALF_DIM), small(HALF_DIM), small(HALF_DIM),
            pl.BlockSpec((HEAD_DIM, 1), lambda h, qi, b: (0, 0)),
            pl.BlockSpec((tq, HEAD_DIM), lambda h, qi, b: (b * q_tiles + qi, h)),
            pl.BlockSpec((seq, HEAD_DIM), lambda h, qi, b: (b, N_HEADS + h)),
            pl.BlockSpec((chunks, HEAD_DIM, kc), lambda h, qi, b: (b, h, 0)),
        ],
        out_specs=pl.BlockSpec((tq, HEAD_DIM), lambda h, qi, b: (b * q_tiles + qi, h)),
        out_shape=jax.ShapeDtypeStruct((n, d), BF16),
        scratch_shapes=[
            pltpu.VMEM((seq, tq), F32),
            pltpu.VMEM((seq, 2 * tq), F32),
        ],
        compiler_params=_params(3),
        name="diff_attention",
    )(vec(lq1), vec(lk1), vec(lq2), vec(lk2), subln.reshape(HEAD_DIM, 1).astype(F32), qk, qk, vt)


def _proj_residual_kernel(a_ref, x_ref, w_ref, o_ref):
    o_ref[...] = x_ref[...] + jnp.dot(a_ref[...], w_ref[...], preferred_element_type=F32)


def _proj_residual(a, x, w, *, tm=512):
    n, d = x.shape
    return pl.pallas_call(
        _proj_residual_kernel,
        grid=(n // tm,),
        in_specs=[
            pl.BlockSpec((tm, d), lambda i: (i, 0)),
            pl.BlockSpec((tm, d), lambda i: (i, 0)),
            pl.BlockSpec((d, d), lambda i: (0, 0)),
        ],
        out_specs=pl.BlockSpec((tm, d), lambda i: (i, 0)),
        out_shape=jax.ShapeDtypeStruct((n, d), F32),
        compiler_params=_params(1),
        name="proj_residual",
    )(a, x, w)


def kernel(x, ln_mix, ln_mlp, conv_w_in, conv_w, conv_w_out, attn_w_qkv, attn_lambda_q1,
           attn_lambda_k1, attn_lambda_q2, attn_lambda_k2, attn_subln, attn_w_o, mlp_w1,
           mlp_w2, ln_f):
    batch, seq, d = x.shape
    depth = ln_mix.shape[0]
    h = x.reshape(batch * seq, d)
    for i in range(depth):
        j = i // N_MIXERS
        if i % N_MIXERS == 0:
            bcu, _ = _norm_matmul(h, ln_mix[i], conv_w_in[j].astype(BF16))
            h = _conv_mix(bcu, h, conv_w[j], conv_w_out[j].astype(BF16), seq=seq)
        else:
            w_qkv = attn_w_qkv[j]
            qk, hn = _norm_matmul(h, ln_mix[i], w_qkv[:, :2 * d].astype(BF16),
                                  scaled_cols=d, scale=ATTN_SCALE * LOG2E)
            vt = _matmul_t(hn, w_qkv[:, 2 * d:].T.astype(BF16), chunk=ATTN_KEY_CHUNK)
            o = _attention(qk, vt, attn_lambda_q1[j], attn_lambda_k1[j], attn_lambda_q2[j],
                           attn_lambda_k2[j], attn_subln[j], batch=batch, seq=seq,
                           lambda_init=_lambda_init(i))
            h = _proj_residual(o, h, attn_w_o[j].astype(BF16))
        g_final = ln_f if i == depth - 1 else None
        h = _mlp(h, ln_mlp[i], mlp_w1[i].astype(BF16), mlp_w2[i].astype(BF16), g_final)
    return h.reshape(batch, seq, d)
```

```python
import functools
import math

import jax
import jax.numpy as jnp
from jax import lax
from jax.experimental import pallas as pl
from jax.experimental.pallas import tpu as pltpu

N_HEADS = 16
HEAD_DIM = 128
HALF_DIM = HEAD_DIM // 2
ATTN_SCALE = HALF_DIM ** -0.5
LOG2E = math.log2(math.e)
NORM_EPS = 1e-6
SUBLN_EPS = 1e-5
N_MIXERS = 2

V7X_VMEM_BYTES = 64 * 1024 * 1024
VMEM_LIMIT_BYTES = V7X_VMEM_BYTES - 8 * 1024 * 1024
SUBLANES = 8
BF16_SUBLANES = 16
NORM_ROWS = 256
ATTN_KEY_CHUNK = 512

F32 = jnp.float32
BF16 = jnp.bfloat16
NT_DIMS = (((1,), (1,)), ((), ()))


def _lambda_init(layer_idx):
    return 0.8 - 0.6 * math.exp(-0.3 * layer_idx)


def _rmsnorm(x, g, eps):
    ms = jnp.mean(x * x, axis=-1, keepdims=True)
    return x * lax.rsqrt(ms + eps) * g


def _norm_rows_to(dst_ref, src_ref, g_ref, eps):
    rows = src_ref.shape[0]
    g = g_ref[...]

    def body(c, carry):
        sl = pl.ds(pl.multiple_of(c * NORM_ROWS, NORM_ROWS), NORM_ROWS)
        dst_ref[sl, :] = _rmsnorm(src_ref[sl, :], g, eps).astype(dst_ref.dtype)
        return carry

    lax.fori_loop(0, rows // NORM_ROWS, body, 0)


def _params(n_axes):
    return pltpu.CompilerParams(
        dimension_semantics=("arbitrary",) * n_axes,
        vmem_limit_bytes=VMEM_LIMIT_BYTES)


def _norm_matmul_kernel(x_ref, g_ref, w_ref, o_ref, hn_ref, *, scaled_tiles, scale):
    j = pl.program_id(1)

    @pl.when(j == 0)
    def _():
        _norm_rows_to(hn_ref, x_ref, g_ref, NORM_EPS)

    acc = jnp.dot(hn_ref[...], w_ref[...], preferred_element_type=F32)
    if scaled_tiles:
        acc = acc * jnp.where(j < scaled_tiles, scale, 1.0).astype(F32)
    o_ref[...] = acc.astype(o_ref.dtype)


def _norm_matmul(x, g, w, *, scaled_cols=0, scale=1.0, tm=1024, tn=512):
    n, d = x.shape
    n_out = w.shape[1]
    return pl.pallas_call(
        functools.partial(_norm_matmul_kernel, scaled_tiles=scaled_cols // tn, scale=scale),
        grid=(n // tm, n_out // tn),
        in_specs=[
            pl.BlockSpec((tm, d), lambda i, j: (i, 0)),
            pl.BlockSpec((1, d), lambda i, j: (0, 0)),
            pl.BlockSpec((d, tn), lambda i, j: (0, j)),
        ],
        out_specs=[pl.BlockSpec((tm, tn), lambda i, j: (i, j)),
                   pl.BlockSpec((tm, d), lambda i, j: (i, 0))],
        out_shape=[jax.ShapeDtypeStruct((n, n_out), BF16),
                   jax.ShapeDtypeStruct((n, d), BF16)],
        compiler_params=_params(2),
        name="norm_matmul",
    )(x, g.reshape(1, d), w)


def _matmul_t_kernel(a_ref, wt_ref, o_ref):
    chunk = o_ref.shape[2]
    out = lax.dot_general(wt_ref[...], a_ref[...], NT_DIMS,
                          preferred_element_type=F32).astype(o_ref.dtype)
    for c in range(o_ref.shape[0]):
        o_ref[c] = out[:, c * chunk:(c + 1) * chunk]


def _matmul_t(a, wt, *, chunk, tm=1024, tn=1024):
    n, d = a.shape
    n_out = wt.shape[0]
    return pl.pallas_call(
        _matmul_t_kernel,
        grid=(n // tm, n_out // tn),
        in_specs=[
            pl.BlockSpec((tm, d), lambda i, j: (i, 0)),
            pl.BlockSpec((tn, d), lambda i, j: (j, 0)),
        ],
        out_specs=pl.BlockSpec((tm // chunk, tn, chunk), lambda i, j: (i, j, 0)),
        out_shape=jax.ShapeDtypeStruct((n // chunk, n_out, chunk), BF16),
        compiler_params=_params(2),
        name="matmul_t",
    )(a, wt)


def _conv_mix_kernel(b_ref, c_ref, u_ref, cp_ref, up_ref, cn_ref, un_ref,
                     x_ref, cw_ref, wo_ref, o_ref, *, seq):
    i = pl.program_id(0)
    tm = b_ref.shape[0]
    cu = c_ref[...].astype(F32) * u_ref[...].astype(F32)
    last = BF16_SUBLANES - 1
    prev = cp_ref[last:, :].astype(F32) * up_ref[last:, :].astype(F32)
    nxt = cn_ref[:1, :].astype(F32) * un_ref[:1, :].astype(F32)
    prev = jnp.where((i * tm) % seq == 0, 0.0, prev)
    nxt = jnp.where(((i + 1) * tm) % seq == 0, 0.0, nxt)
    row = lax.broadcasted_iota(jnp.int32, cu.shape, 0)
    up = jnp.where(row == 0, prev, pltpu.roll(cu, 1, axis=0))
    dn = jnp.where(row == tm - 1, nxt, pltpu.roll(cu, tm - 1, axis=0))
    cw = cw_ref[...]
    conv = cw[0:1, :] * up + cw[1:2, :] * cu + cw[2:3, :] * dn
    g = (b_ref[...].astype(F32) * conv).astype(BF16)
    o_ref[...] = x_ref[...] + jnp.dot(g, wo_ref[...], preferred_element_type=F32)


def _conv_mix(bcu, x, conv_w, w_out, *, seq, tm=512):
    n, d = x.shape
    hb = BF16_SUBLANES
    tiles_per_halo = tm // hb
    n_halo = n // hb
    prev_idx = lambda i: jnp.maximum(i * tiles_per_halo - 1, 0)
    next_idx = lambda i: jnp.minimum((i + 1) * tiles_per_halo, n_halo - 1)
    return pl.pallas_call(
        functools.partial(_conv_mix_kernel, seq=seq),
        grid=(n // tm,),
        in_specs=[
            pl.BlockSpec((tm, d), lambda i: (i, 0)),
            pl.BlockSpec((tm, d), lambda i: (i, 1)),
            pl.BlockSpec((tm, d), lambda i: (i, 2)),
            pl.BlockSpec((hb, d), lambda i: (prev_idx(i), 1)),
            pl.BlockSpec((hb, d), lambda i: (prev_idx(i), 2)),
            pl.BlockSpec((hb, d), lambda i: (next_idx(i), 1)),
            pl.BlockSpec((hb, d), lambda i: (next_idx(i), 2)),
            pl.BlockSpec((tm, d), lambda i: (i, 0)),
            pl.BlockSpec((3, d), lambda i: (0, 0)),
            pl.BlockSpec((d, d), lambda i: (0, 0)),
        ],
        out_specs=pl.BlockSpec((tm, d), lambda i: (i, 0)),
        out_shape=jax.ShapeDtypeStruct((n, d), F32),
        compiler_params=_params(1),
        name="conv_mix",
    )(bcu, bcu, bcu, bcu, bcu, bcu, bcu, x, conv_w, w_out)


def _mlp_kernel(x_ref, g_ref, w1_ref, w2_ref, gf_ref, o_ref, hn_ref, *, final_norm):
    f = pl.program_id(1)

    @pl.when(f == 0)
    def _():
        _norm_rows_to(hn_ref, x_ref, g_ref, NORM_EPS)
        o_ref[...] = x_ref[...]

    a = jnp.dot(hn_ref[...], w1_ref[...], preferred_element_type=F32)
    a = jnp.maximum(a, 0.0)
    a = (a * a).astype(BF16)
    o_ref[...] += jnp.dot(a, w2_ref[...], preferred_element_type=F32)

    if final_norm:
        @pl.when(f == pl.num_programs(1) - 1)
        def _():
            _norm_rows_to(o_ref, o_ref, gf_ref, NORM_EPS)


def _mlp(x, g, w1, w2, g_final, *, tm=1024, tf=512):
    n, d = x.shape
    d_ff = w1.shape[1]
    final_norm = g_final is not None
    gf = (g_final if final_norm else g).reshape(1, d)
    return pl.pallas_call(
        functools.partial(_mlp_kernel, final_norm=final_norm),
        grid=(n // tm, d_ff // tf),
        in_specs=[
            pl.BlockSpec((tm, d), lambda i, f: (i, 0)),
            pl.BlockSpec((1, d), lambda i, f: (0, 0)),
            pl.BlockSpec((d, tf), lambda i, f: (0, f)),
            pl.BlockSpec((tf, d), lambda i, f: (f, 0)),
            pl.BlockSpec((1, d), lambda i, f: (0, 0)),
        ],
        out_specs=pl.BlockSpec((tm, d), lambda i, f: (i, 0)),
        out_shape=jax.ShapeDtypeStruct((n, d), F32),
        scratch_shapes=[pltpu.VMEM((tm, d), BF16)],
        compiler_params=_params(2),
        name="mlp",
    )(x, g.reshape(1, d), w1, w2, gf)


def _attn_tile_ids(t, batch, q_tiles):
    return t // (batch * q_tiles), (t // batch) % q_tiles, t % batch


def _attn_kernel(lq1_ref, lk1_ref, lq2_ref, lk2_ref, subln_ref, q_ref, k_ref, vt_ref,
                 o_ref, bias_ref, t0_ref, t1_ref, m0_ref, m1_ref, *, batch, q_tiles, lambda_init):
    n = pl.program_id(0)
    n_tiles = pl.num_programs(0) - 1
    h, qi, b = _attn_tile_ids(jnp.minimum(n, n_tiles - 1), batch, q_tiles)
    tq = q_ref.shape[0]
    n_kc, _, kc = vt_ref.shape
    w = 2 * tq

    @pl.when(n == 0)
    def _():
        t1_ref[...] = jnp.zeros_like(t1_ref)
        m1_ref[...] = jnp.zeros_like(m1_ref)

    @pl.when(jnp.logical_and(b == 0, n < n_tiles))
    def _():
        slope = jnp.exp2(jnp.full((1, 1), -8.0 / N_HEADS, F32) * (h + 1).astype(F32))
        neg_slope = -slope * LOG2E

        def body(j, carry):
            key = j * kc + lax.broadcasted_iota(jnp.int32, (kc, tq), 0)
            qry = qi * tq + lax.broadcasted_iota(jnp.int32, (kc, tq), 1)
            rows = pl.ds(pl.multiple_of(j * kc, kc), kc)
            bias_ref[rows, :] = neg_slope * jnp.abs(qry - key).astype(F32)
            return carry

        lax.fori_loop(0, n_kc, body, 0)

    def step(t_cur, m_cur, t_prev, m_prev):
        q = q_ref[...]
        lane = lax.broadcasted_iota(jnp.int32, q.shape, 1)
        zero = jnp.zeros_like(q)
        q_both = jnp.concatenate([jnp.where(lane < HALF_DIM, q, zero),
                                  jnp.where(lane >= HALF_DIM, q, zero)], axis=0)
        m_run = jnp.full((SUBLANES, w), -1e30, F32)

        def pass1_chunk(j, m_run):
            s = lax.dot_general(k_ref[j * kc:(j + 1) * kc, :], q_both, NT_DIMS,
                                preferred_element_type=F32)
            for r in range(kc // SUBLANES):
                rows = slice(j * kc + r * SUBLANES, j * kc + (r + 1) * SUBLANES)
                br = bias_ref[rows, :]
                t = s[r * SUBLANES:(r + 1) * SUBLANES, :] + jnp.concatenate([br, br], axis=1)
                t_cur[rows, :] = t
                m_run = jnp.maximum(m_run, t)
            return m_run

        ones_rows = (lax.broadcasted_iota(jnp.int32, (BF16_SUBLANES, kc), 0) == 0).astype(BF16)
        m = m_prev[...]
        slab = BF16_SUBLANES

        def pass2_chunk(j):
            parts = []
            for r in range(kc // slab):
                rows = slice(j * kc + r * slab, j * kc + (r + 1) * slab)
                x = t_prev[rows, :].reshape(slab // SUBLANES, SUBLANES, w) - m[None]
                parts.append(jnp.exp2(x.reshape(slab, w).astype(BF16)))
            e = jnp.concatenate(parts, axis=0)
            v_aug = jnp.concatenate([vt_ref[j], ones_rows], axis=0)
            return jnp.dot(v_aug, e, preferred_element_type=F32)

        acc = None
        for j in range(n_kc):
            m_run = pass1_chunk(j, m_run)
            pv = pass2_chunk(j)
            acc = pv if acc is None else acc + pv
        m_cur[...] = jnp.broadcast_to(jnp.max(m_run, axis=0, keepdims=True), (SUBLANES, w))

        a0, a1 = acc[:HEAD_DIM, :tq], acc[:HEAD_DIM, tq:]
        l0, l1 = acc[HEAD_DIM:HEAD_DIM + 1, :tq], acc[HEAD_DIM:HEAD_DIM + 1, tq:]
        lam = (jnp.exp(jnp.sum(lq1_ref[...] * lk1_ref[...], keepdims=True))
               - jnp.exp(jnp.sum(lq2_ref[...] * lk2_ref[...], keepdims=True)) + lambda_init)
        o_t = a0 / l0 - lam * (a1 / l1)
        ms = jnp.mean(o_t * o_t, axis=0, keepdims=True)
        o_t = o_t * lax.rsqrt(ms + SUBLN_EPS) * subln_ref[...] * (1.0 - lambda_init)
        o_ref[...] = o_t.T.astype(o_ref.dtype)

    @pl.when(n % 2 == 0)
    def _():
        step(t0_ref, m0_ref, t1_ref, m1_ref)

    @pl.when(n % 2 == 1)
    def _():
        step(t1_ref, m1_ref, t0_ref, m0_ref)


def _attention(qk, vt, lq1, lk1, lq2, lk2, subln, *, batch, seq, lambda_init, tq=256):
    n = qk.shape[0]
    d = N_HEADS * HEAD_DIM
    kc = vt.shape[2]
    chunks = seq // kc
    q_tiles = seq // tq
    n_tiles = N_HEADS * q_tiles * batch
    cur = lambda s: _attn_tile_ids(jnp.minimum(s, n_tiles - 1), batch, q_tiles)
    prev = lambda s: _attn_tile_ids(jnp.maximum(s - 1, 0), batch, q_tiles)

    def q_map(s):
        h, qi, b = cur(s)
        return b * q_tiles + qi, h

    def k_map(s):
        h, _, b = cur(s)
        return b, N_HEADS + h

    def vt_map(s):
        h, _, b = prev(s)
        return b, h, 0

    def o_map(s):
        h, qi, b = prev(s)
        return b * q_tiles + qi, h

    vec = lambda a: a.reshape(1, -1).astype(F32)
    small = lambda width: pl.BlockSpec((1, width), lambda s: (0, 0))
    return pl.pallas_call(
        functools.partial(_attn_kernel, batch=batch, q_tiles=q_tiles, lambda_init=lambda_init),
        grid=(n_tiles + 1,),
        in_specs=[
            small(HALF_DIM), small(HALF_DIM), small(HALF_DIM), small(HALF_DIM),
            pl.BlockSpec((HEAD_DIM, 1), lambda s: (0, 0)),
            pl.BlockSpec((tq, HEAD_DIM), q_map),
            pl.BlockSpec((seq, HEAD_DIM), k_map),
            pl.BlockSpec((chunks, HEAD_DIM, kc), vt_map),
        ],
        out_specs=pl.BlockSpec((tq, HEAD_DIM), o_map),
        out_shape=jax.ShapeDtypeStruct((n, d), BF16),
        scratch_shapes=[
            pltpu.VMEM((seq, tq), F32),
            pltpu.VMEM((seq, 2 * tq), F32),
            pltpu.VMEM((seq, 2 * tq), F32),
            pltpu.VMEM((SUBLANES, 2 * tq), F32),
            pltpu.VMEM((SUBLANES, 2 * tq), F32),
        ],
        compiler_params=_params(1),
        name="diff_attention",
    )(vec(lq1), vec(lk1), vec(lq2), vec(lk2), subln.reshape(HEAD_DIM, 1).astype(F32), qk, qk, vt)


def _proj_residual_kernel(a_ref, x_ref, w_ref, o_ref):
    o_ref[...] = x_ref[...] + jnp.dot(a_ref[...], w_ref[...], preferred_element_type=F32)


def _proj_residual(a, x, w, *, tm=512):
    n, d = x.shape
    return pl.pallas_call(
        _proj_residual_kernel,
        grid=(n // tm,),
        in_specs=[
            pl.BlockSpec((tm, d), lambda i: (i, 0)),
            pl.BlockSpec((tm, d), lambda i: (i, 0)),
            pl.BlockSpec((d, d), lambda i: (0, 0)),
        ],
        out_specs=pl.BlockSpec((tm, d), lambda i: (i, 0)),
        out_shape=jax.ShapeDtypeStruct((n, d), F32),
        compiler_params=_params(1),
        name="proj_residual",
    )(a, x, w)


def kernel(x, ln_mix, ln_mlp, conv_w_in, conv_w, conv_w_out, attn_w_qkv, attn_lambda_q1,
           attn_lambda_k1, attn_lambda_q2, attn_lambda_k2, attn_subln, attn_w_o, mlp_w1,
           mlp_w2, ln_f):
    batch, seq, d = x.shape
    depth = ln_mix.shape[0]
    h = x.reshape(batch * seq, d)
    for i in range(depth):
        j = i // N_MIXERS
        if i % N_MIXERS == 0:
            bcu, _ = _norm_matmul(h, ln_mix[i], conv_w_in[j].astype(BF16))
            h = _conv_mix(bcu, h, conv_w[j], conv_w_out[j].astype(BF16), seq=seq)
        else:
            w_qkv = attn_w_qkv[j]
            qk, hn = _norm_matmul(h, ln_mix[i], w_qkv[:, :2 * d].astype(BF16),
                                  scaled_cols=d, scale=ATTN_SCALE * LOG2E)
            vt = _matmul_t(hn, w_qkv[:, 2 * d:].T.astype(BF16), chunk=ATTN_KEY_CHUNK)
            o = _attention(qk, vt, attn_lambda_q1[j], attn_lambda_k1[j], attn_lambda_q2[j],
                           attn_lambda_k2[j], attn_subln[j], batch=batch, seq=seq,
                           lambda_init=_lambda_init(i))
            h = _proj_residual(o, h, attn_w_o[j].astype(BF16))
        g_final = ln_f if i == depth - 1 else None
        h = _mlp(h, ln_mlp[i], mlp_w1[i].astype(BF16), mlp_w2[i].astype(BF16), g_final)
    return h.reshape(batch, seq, d)
```

```python
import functools
import math

import jax
import jax.numpy as jnp
from jax import lax
from jax.experimental import pallas as pl
from jax.experimental.pallas import tpu as pltpu

N_HEADS = 16
HEAD_DIM = 128
HALF_DIM = HEAD_DIM // 2
ATTN_SCALE = HALF_DIM ** -0.5
LOG2E = math.log2(math.e)
NORM_EPS = 1e-6
SUBLN_EPS = 1e-5
N_MIXERS = 2

V7X_VMEM_BYTES = 64 * 1024 * 1024
VMEM_LIMIT_BYTES = V7X_VMEM_BYTES - 8 * 1024 * 1024
SUBLANES = 8
BF16_SUBLANES = 16
NORM_ROWS = 256
ATTN_KEY_CHUNK = 512
ATTN_PIPELINE_DRAIN_STEPS = 2

F32 = jnp.float32
BF16 = jnp.bfloat16
NT_DIMS = (((1,), (1,)), ((), ()))


def _lambda_init(layer_idx):
    return 0.8 - 0.6 * math.exp(-0.3 * layer_idx)


def _rmsnorm(x, g, eps):
    ms = jnp.mean(x * x, axis=-1, keepdims=True)
    return x * lax.rsqrt(ms + eps) * g


def _norm_rows_to(dst_ref, src_ref, g_ref, eps):
    rows = src_ref.shape[0]
    g = g_ref[...]

    def body(c, carry):
        sl = pl.ds(pl.multiple_of(c * NORM_ROWS, NORM_ROWS), NORM_ROWS)
        dst_ref[sl, :] = _rmsnorm(src_ref[sl, :], g, eps).astype(dst_ref.dtype)
        return carry

    lax.fori_loop(0, rows // NORM_ROWS, body, 0)


def _params(n_axes):
    return pltpu.CompilerParams(
        dimension_semantics=("arbitrary",) * n_axes,
        vmem_limit_bytes=VMEM_LIMIT_BYTES)


def _norm_matmul_kernel(x_ref, g_ref, w_ref, o_ref, hn_ref, *, scaled_tiles, scale):
    j = pl.program_id(1)

    @pl.when(j == 0)
    def _():
        _norm_rows_to(hn_ref, x_ref, g_ref, NORM_EPS)

    acc = jnp.dot(hn_ref[...], w_ref[...], preferred_element_type=F32)
    if scaled_tiles:
        acc = acc * jnp.where(j < scaled_tiles, scale, 1.0).astype(F32)
    o_ref[...] = acc.astype(o_ref.dtype)


def _norm_matmul(x, g, w, *, emit_normed=False, scaled_cols=0, scale=1.0, tm=1024, tn=1024):
    n, d = x.shape
    n_out = w.shape[1]
    out_specs = [pl.BlockSpec((tm, tn), lambda i, j: (i, j))]
    out_shape = [jax.ShapeDtypeStruct((n, n_out), BF16)]
    scratch_shapes = []
    if emit_normed:
        out_specs.append(pl.BlockSpec((tm, d), lambda i, j: (i, 0)))
        out_shape.append(jax.ShapeDtypeStruct((n, d), BF16))
    else:
        scratch_shapes.append(pltpu.VMEM((tm, d), BF16))
    outs = pl.pallas_call(
        functools.partial(_norm_matmul_kernel, scaled_tiles=scaled_cols // tn, scale=scale),
        grid=(n // tm, n_out // tn),
        in_specs=[
            pl.BlockSpec((tm, d), lambda i, j: (i, 0)),
            pl.BlockSpec((1, d), lambda i, j: (0, 0)),
            pl.BlockSpec((d, tn), lambda i, j: (0, j)),
        ],
        out_specs=out_specs,
        out_shape=out_shape,
        scratch_shapes=scratch_shapes,
        compiler_params=_params(2),
        name="norm_matmul",
    )(x, g.reshape(1, d), w)
    return outs if emit_normed else outs[0]


def _matmul_t_kernel(a_ref, wt_ref, o_ref):
    chunk = o_ref.shape[2]
    out = lax.dot_general(wt_ref[...], a_ref[...], NT_DIMS,
                          preferred_element_type=F32).astype(o_ref.dtype)
    for c in range(o_ref.shape[0]):
        o_ref[c] = out[:, c * chunk:(c + 1) * chunk]


def _matmul_t(a, wt, *, chunk, tm=1024, tn=1024):
    n, d = a.shape
    n_out = wt.shape[0]
    return pl.pallas_call(
        _matmul_t_kernel,
        grid=(n // tm, n_out // tn),
        in_specs=[
            pl.BlockSpec((tm, d), lambda i, j: (i, 0)),
            pl.BlockSpec((tn, d), lambda i, j: (j, 0)),
        ],
        out_specs=pl.BlockSpec((tm // chunk, tn, chunk), lambda i, j: (i, j, 0)),
        out_shape=jax.ShapeDtypeStruct((n // chunk, n_out, chunk), BF16),
        compiler_params=_params(2),
        name="matmul_t",
    )(a, wt)


def _conv_mix_kernel(b_ref, c_ref, u_ref, cp_ref, up_ref, cn_ref, un_ref,
                     x_ref, cw_ref, wo_ref, o_ref, *, seq):
    i = pl.program_id(0)
    tm = b_ref.shape[0]
    cu = c_ref[...].astype(F32) * u_ref[...].astype(F32)
    last = BF16_SUBLANES - 1
    prev = cp_ref[last:, :].astype(F32) * up_ref[last:, :].astype(F32)
    nxt = cn_ref[:1, :].astype(F32) * un_ref[:1, :].astype(F32)
    prev = jnp.where((i * tm) % seq == 0, 0.0, prev)
    nxt = jnp.where(((i + 1) * tm) % seq == 0, 0.0, nxt)
    row = lax.broadcasted_iota(jnp.int32, cu.shape, 0)
    up = jnp.where(row == 0, prev, pltpu.roll(cu, 1, axis=0))
    dn = jnp.where(row == tm - 1, nxt, pltpu.roll(cu, tm - 1, axis=0))
    cw = cw_ref[...]
    conv = cw[0:1, :] * up + cw[1:2, :] * cu + cw[2:3, :] * dn
    g = (b_ref[...].astype(F32) * conv).astype(BF16)
    o_ref[...] = x_ref[...] + jnp.dot(g, wo_ref[...], preferred_element_type=F32)


def _conv_mix(bcu, x, conv_w, w_out, *, seq, tm=512):
    n, d = x.shape
    hb = BF16_SUBLANES
    tiles_per_halo = tm // hb
    n_halo = n // hb
    prev_idx = lambda i: jnp.maximum(i * tiles_per_halo - 1, 0)
    next_idx = lambda i: jnp.minimum((i + 1) * tiles_per_halo, n_halo - 1)
    return pl.pallas_call(
        functools.partial(_conv_mix_kernel, seq=seq),
        grid=(n // tm,),
        in_specs=[
            pl.BlockSpec((tm, d), lambda i: (i, 0)),
            pl.BlockSpec((tm, d), lambda i: (i, 1)),
            pl.BlockSpec((tm, d), lambda i: (i, 2)),
            pl.BlockSpec((hb, d), lambda i: (prev_idx(i), 1)),
            pl.BlockSpec((hb, d), lambda i: (prev_idx(i), 2)),
            pl.BlockSpec((hb, d), lambda i: (next_idx(i), 1)),
            pl.BlockSpec((hb, d), lambda i: (next_idx(i), 2)),
            pl.BlockSpec((tm, d), lambda i: (i, 0)),
            pl.BlockSpec((3, d), lambda i: (0, 0)),
            pl.BlockSpec((d, d), lambda i: (0, 0)),
        ],
        out_specs=pl.BlockSpec((tm, d), lambda i: (i, 0)),
        out_shape=jax.ShapeDtypeStruct((n, d), F32),
        compiler_params=_params(1),
        name="conv_mix",
    )(bcu, bcu, bcu, bcu, bcu, bcu, bcu, x, conv_w, w_out)


def _mlp_kernel(x_ref, g_ref, w1_ref, w2_ref, gf_ref, o_ref, hn_ref, *, final_norm):
    f = pl.program_id(1)

    @pl.when(f == 0)
    def _():
        _norm_rows_to(hn_ref, x_ref, g_ref, NORM_EPS)
        o_ref[...] = x_ref[...]

    a = jnp.dot(hn_ref[...], w1_ref[...], preferred_element_type=F32)
    a = jnp.maximum(a, 0.0)
    a = (a * a).astype(BF16)
    o_ref[...] += jnp.dot(a, w2_ref[...], preferred_element_type=F32)

    if final_norm:
        @pl.when(f == pl.num_programs(1) - 1)
        def _():
            _norm_rows_to(o_ref, o_ref, gf_ref, NORM_EPS)


def _mlp(x, g, w1, w2, g_final, *, tm=1024, tf=512):
    n, d = x.shape
    d_ff = w1.shape[1]
    final_norm = g_final is not None
    gf = (g_final if final_norm else g).reshape(1, d)
    return pl.pallas_call(
        functools.partial(_mlp_kernel, final_norm=final_norm),
        grid=(n // tm, d_ff // tf),
        in_specs=[
            pl.BlockSpec((tm, d), lambda i, f: (i, 0)),
            pl.BlockSpec((1, d), lambda i, f: (0, 0)),
            pl.BlockSpec((d, tf), lambda i, f: (0, f)),
            pl.BlockSpec((tf, d), lambda i, f: (f, 0)),
            pl.BlockSpec((1, d), lambda i, f: (0, 0)),
        ],
        out_specs=pl.BlockSpec((tm, d), lambda i, f: (i, 0)),
        out_shape=jax.ShapeDtypeStruct((n, d), F32),
        scratch_shapes=[pltpu.VMEM((tm, d), BF16)],
        compiler_params=_params(2),
        name="mlp",
    )(x, g.reshape(1, d), w1, w2, gf)


def _attn_tile_ids(t, batch, q_tiles):
    return t // (batch * q_tiles), (t // batch) % q_tiles, t % batch


def _attn_kernel(lq1_ref, lk1_ref, lq2_ref, lk2_ref, subln_ref, q_ref, k_ref, vt_ref,
                 o_ref, rel_ref, t0_ref, t1_ref, m0_ref, m1_ref, acc0_ref, acc1_ref,
                 *, batch, q_tiles, lambda_init):
    n = pl.program_id(0)
    n_tiles = pl.num_programs(0) - ATTN_PIPELINE_DRAIN_STEPS
    h, qi, b = _attn_tile_ids(jnp.minimum(n, n_tiles - 1), batch, q_tiles)
    tq = q_ref.shape[0]
    seq = k_ref.shape[0]
    n_kc, _, kc = vt_ref.shape
    w = 2 * tq

    @pl.when(n == 0)
    def _():
        t1_ref[...] = jnp.zeros_like(t1_ref)
        m1_ref[...] = jnp.zeros_like(m1_ref)
        acc0_ref[...] = jnp.ones_like(acc0_ref)
        acc1_ref[...] = jnp.ones_like(acc1_ref)

    @pl.when(jnp.logical_and(qi == 0, b == 0))
    def _():
        slope = jnp.exp2(jnp.full((1, 1), -8.0 / N_HEADS, F32) * (h + 1).astype(F32))
        neg_slope = -slope * LOG2E

        def body(c, carry):
            u = c * tq + lax.broadcasted_iota(jnp.int32, (tq, tq), 0)
            col = lax.broadcasted_iota(jnp.int32, (tq, tq), 1)
            rows = pl.ds(pl.multiple_of(c * tq, tq), tq)
            rel_ref[rows, :] = neg_slope * jnp.abs(col - u + (seq - tq)).astype(F32)
            return carry

        lax.fori_loop(0, rel_ref.shape[0] // tq, body, 0)

    bias_base = (seq - tq) - qi * tq

    def step(t_cur, m_cur, acc_cur, t_prev, m_prev, acc_prev):
        acc = acc_prev[...]
        a0, a1 = acc[:HEAD_DIM, :tq], acc[:HEAD_DIM, tq:]
        l0, l1 = acc[HEAD_DIM:HEAD_DIM + 1, :tq], acc[HEAD_DIM:HEAD_DIM + 1, tq:]
        lam = (jnp.exp(jnp.sum(lq1_ref[...] * lk1_ref[...], keepdims=True))
               - jnp.exp(jnp.sum(lq2_ref[...] * lk2_ref[...], keepdims=True)) + lambda_init)
        o_t = a0 / l0 - lam * (a1 / l1)
        ms = jnp.mean(o_t * o_t, axis=0, keepdims=True)
        o_t = o_t * lax.rsqrt(ms + SUBLN_EPS) * subln_ref[...] * (1.0 - lambda_init)
        o_ref[...] = o_t.T.astype(o_ref.dtype)

        q = q_ref[...]
        lane = lax.broadcasted_iota(jnp.int32, q.shape, 1)
        zero = jnp.zeros_like(q)
        q_both = jnp.concatenate([jnp.where(lane < HALF_DIM, q, zero),
                                  jnp.where(lane >= HALF_DIM, q, zero)], axis=0)
        m_run = jnp.full((SUBLANES, w), -1e30, F32)

        def pass1_chunk(j, m_run):
            s = lax.dot_general(k_ref[j * kc:(j + 1) * kc, :], q_both, NT_DIMS,
                                preferred_element_type=F32)
            for r in range(kc // SUBLANES):
                rows = slice(j * kc + r * SUBLANES, j * kc + (r + 1) * SUBLANES)
                rel_rows = pl.ds(pl.multiple_of(bias_base + rows.start, SUBLANES), SUBLANES)
                br = rel_ref[rel_rows, :]
                t = s[r * SUBLANES:(r + 1) * SUBLANES, :] + jnp.concatenate([br, br], axis=1)
                t_cur[rows, :] = t
                m_run = jnp.maximum(m_run, t)
            return m_run

        ones_rows = (lax.broadcasted_iota(jnp.int32, (BF16_SUBLANES, kc), 0) == 0).astype(BF16)
        m = m_prev[...]
        slab = BF16_SUBLANES

        def pass2_chunk(j):
            parts = []
            for r in range(kc // slab):
                rows = slice(j * kc + r * slab, j * kc + (r + 1) * slab)
                x = t_prev[rows, :].reshape(slab // SUBLANES, SUBLANES, w) - m[None]
                parts.append(jnp.exp2(x.reshape(slab, w).astype(BF16)))
            e = jnp.concatenate(parts, axis=0)
            v_aug = jnp.concatenate([vt_ref[j], ones_rows], axis=0)
            return jnp.dot(v_aug, e, preferred_element_type=F32)

        acc = None
        for j in range(n_kc):
            m_run = pass1_chunk(j, m_run)
            pv = pass2_chunk(j)
            acc = pv if acc is None else acc + pv
        m_cur[...] = jnp.broadcast_to(jnp.max(m_run, axis=0, keepdims=True), (SUBLANES, w))
        acc_cur[...] = acc

    @pl.when(n % 2 == 0)
    def _():
        step(t0_ref, m0_ref, acc0_ref, t1_ref, m1_ref, acc1_ref)

    @pl.when(n % 2 == 1)
    def _():
        step(t1_ref, m1_ref, acc1_ref, t0_ref, m0_ref, acc0_ref)


def _attention(qk, vt, lq1, lk1, lq2, lk2, subln, *, batch, seq, lambda_init, tq=256):
    n = qk.shape[0]
    d = N_HEADS * HEAD_DIM
    kc = vt.shape[2]
    chunks = seq // kc
    q_tiles = seq // tq
    n_tiles = N_HEADS * q_tiles * batch
    tile = lambda s, lag: _attn_tile_ids(jnp.clip(s - lag, 0, n_tiles - 1), batch, q_tiles)

    def q_map(s):
        h, qi, b = tile(s, 0)
        return b * q_tiles + qi, h

    def k_map(s):
        h, _, b = tile(s, 0)
        return b, N_HEADS + h

    def vt_map(s):
        h, _, b = tile(s, 1)
        return b, h, 0

    def o_map(s):
        h, qi, b = tile(s, 2)
        return b * q_tiles + qi, h

    vec = lambda a: a.reshape(1, -1).astype(F32)
    small = lambda width: pl.BlockSpec((1, width), lambda s: (0, 0))
    return pl.pallas_call(
        functools.partial(_attn_kernel, batch=batch, q_tiles=q_tiles, lambda_init=lambda_init),
        grid=(n_tiles + ATTN_PIPELINE_DRAIN_STEPS,),
        in_specs=[
            small(HALF_DIM), small(HALF_DIM), small(HALF_DIM), small(HALF_DIM),
            pl.BlockSpec((HEAD_DIM, 1), lambda s: (0, 0)),
            pl.BlockSpec((tq, HEAD_DIM), q_map),
            pl.BlockSpec((seq, HEAD_DIM), k_map),
            pl.BlockSpec((chunks, HEAD_DIM, kc), vt_map),
        ],
        out_specs=pl.BlockSpec((tq, HEAD_DIM), o_map),
        out_shape=jax.ShapeDtypeStruct((n, d), BF16),
        scratch_shapes=[
            pltpu.VMEM((2 * seq - tq, tq), F32),
            pltpu.VMEM((seq, 2 * tq), F32),
            pltpu.VMEM((seq, 2 * tq), F32),
            pltpu.VMEM((SUBLANES, 2 * tq), F32),
            pltpu.VMEM((SUBLANES, 2 * tq), F32),
            pltpu.VMEM((HEAD_DIM + BF16_SUBLANES, 2 * tq), F32),
            pltpu.VMEM((HEAD_DIM + BF16_SUBLANES, 2 * tq), F32),
        ],
        compiler_params=_params(1),
        name="diff_attention",
    )(vec(lq1), vec(lk1), vec(lq2), vec(lk2), subln.reshape(HEAD_DIM, 1).astype(F32), qk, qk, vt)


def _proj_residual_kernel(a_ref, x_ref, w_ref, o_ref):
    o_ref[...] = x_ref[...] + jnp.dot(a_ref[...], w_ref[...], preferred_element_type=F32)


def _proj_residual(a, x, w, *, tm=512):
    n, d = x.shape
    return pl.pallas_call(
        _proj_residual_kernel,
        grid=(n // tm,),
        in_specs=[
            pl.BlockSpec((tm, d), lambda i: (i, 0)),
            pl.BlockSpec((tm, d), lambda i: (i, 0)),
            pl.BlockSpec((d, d), lambda i: (0, 0)),
        ],
        out_specs=pl.BlockSpec((tm, d), lambda i: (i, 0)),
        out_shape=jax.ShapeDtypeStruct((n, d), F32),
        compiler_params=_params(1),
        name="proj_residual",
    )(a, x, w)


def kernel(x, ln_mix, ln_mlp, conv_w_in, conv_w, conv_w_out, attn_w_qkv, attn_lambda_q1,
           attn_lambda_k1, attn_lambda_q2, attn_lambda_k2, attn_subln, attn_w_o, mlp_w1,
           mlp_w2, ln_f):
    batch, seq, d = x.shape
    depth = ln_mix.shape[0]
    h = x.reshape(batch * seq, d)
    for i in range(depth):
        j = i // N_MIXERS
        if i % N_MIXERS == 0:
            bcu = _norm_matmul(h, ln_mix[i], conv_w_in[j].astype(BF16))
            h = _conv_mix(bcu, h, conv_w[j], conv_w_out[j].astype(BF16), seq=seq)
        else:
            w_qkv = attn_w_qkv[j]
            qk, hn = _norm_matmul(h, ln_mix[i], w_qkv[:, :2 * d].astype(BF16), emit_normed=True,
                                  scaled_cols=d, scale=ATTN_SCALE * LOG2E)
            vt = _matmul_t(hn, w_qkv[:, 2 * d:].T.astype(BF16), chunk=ATTN_KEY_CHUNK)
            o = _attention(qk, vt, attn_lambda_q1[j], attn_lambda_k1[j], attn_lambda_q2[j],
                           attn_lambda_k2[j], attn_subln[j], batch=batch, seq=seq,
                           lambda_init=_lambda_init(i))
            h = _proj_residual(o, h, attn_w_o[j].astype(BF16))
        g_final = ln_f if i == depth - 1 else None
        h = _mlp(h, ln_mlp[i], mlp_w1[i].astype(BF16), mlp_w2[i].astype(BF16), g_final)
    return h.reshape(batch, seq, d)
```

```python
import functools
import math

import jax
import jax.numpy as jnp
from jax import lax
from jax.experimental import pallas as pl
from jax.experimental.pallas import tpu as pltpu

N_HEADS = 16
HEAD_DIM = 128
HALF_DIM = HEAD_DIM // 2
ATTN_SCALE = HALF_DIM ** -0.5
LOG2E = math.log2(math.e)
NORM_EPS = 1e-6
SUBLN_EPS = 1e-5
N_MIXERS = 2

V7X_VMEM_BYTES = 64 * 1024 * 1024
VMEM_LIMIT_BYTES = V7X_VMEM_BYTES - 8 * 1024 * 1024
SUBLANES = 8
BF16_SUBLANES = 16
NORM_ROWS = 256
MLP_HIDDEN_GROUP = 512
ATTN_KEY_CHUNK = 512
ATTN_PIPELINE_DRAIN_STEPS = 2

F32 = jnp.float32
BF16 = jnp.bfloat16
NT_DIMS = (((1,), (1,)), ((), ()))


def _lambda_init(layer_idx):
    return 0.8 - 0.6 * math.exp(-0.3 * layer_idx)


def _rmsnorm(x, g, eps):
    ms = jnp.mean(x * x, axis=-1, keepdims=True)
    return x * lax.rsqrt(ms + eps) * g


def _norm_rows_to(dst_ref, src_ref, g_ref, eps):
    rows = src_ref.shape[0]
    g = g_ref[...]

    def body(c, carry):
        sl = pl.ds(pl.multiple_of(c * NORM_ROWS, NORM_ROWS), NORM_ROWS)
        dst_ref[sl, :] = _rmsnorm(src_ref[sl, :], g, eps).astype(dst_ref.dtype)
        return carry

    lax.fori_loop(0, rows // NORM_ROWS, body, 0)


def _params(n_axes):
    return pltpu.CompilerParams(
        dimension_semantics=("arbitrary",) * n_axes,
        vmem_limit_bytes=VMEM_LIMIT_BYTES)


def _norm_matmul_kernel(x_ref, g_ref, w_ref, o_ref, hn_ref, *, scaled_tiles, scale):
    j = pl.program_id(1)

    @pl.when(j == 0)
    def _():
        _norm_rows_to(hn_ref, x_ref, g_ref, NORM_EPS)

    acc = jnp.dot(hn_ref[...], w_ref[...], preferred_element_type=F32)
    if scaled_tiles:
        acc = acc * jnp.where(j < scaled_tiles, scale, 1.0).astype(F32)
    o_ref[...] = acc.astype(o_ref.dtype)


def _norm_matmul(x, g, w, *, emit_normed=False, scaled_cols=0, scale=1.0, tm=1024, tn=1024):
    n, d = x.shape
    n_out = w.shape[1]
    out_specs = [pl.BlockSpec((tm, tn), lambda i, j: (i, j))]
    out_shape = [jax.ShapeDtypeStruct((n, n_out), BF16)]
    scratch_shapes = []
    if emit_normed:
        out_specs.append(pl.BlockSpec((tm, d), lambda i, j: (i, 0)))
        out_shape.append(jax.ShapeDtypeStruct((n, d), BF16))
    else:
        scratch_shapes.append(pltpu.VMEM((tm, d), BF16))
    outs = pl.pallas_call(
        functools.partial(_norm_matmul_kernel, scaled_tiles=scaled_cols // tn, scale=scale),
        grid=(n // tm, n_out // tn),
        in_specs=[
            pl.BlockSpec((tm, d), lambda i, j: (i, 0)),
            pl.BlockSpec((1, d), lambda i, j: (0, 0)),
            pl.BlockSpec((d, tn), lambda i, j: (0, j)),
        ],
        out_specs=out_specs,
        out_shape=out_shape,
        scratch_shapes=scratch_shapes,
        compiler_params=_params(2),
        name="norm_matmul",
    )(x, g.reshape(1, d), w)
    return outs if emit_normed else outs[0]


def _matmul_t_kernel(a_ref, wt_ref, o_ref):
    chunk = o_ref.shape[2]
    out = lax.dot_general(wt_ref[...], a_ref[...], NT_DIMS,
                          preferred_element_type=F32).astype(o_ref.dtype)
    for c in range(o_ref.shape[0]):
        o_ref[c] = out[:, c * chunk:(c + 1) * chunk]


def _matmul_t(a, wt, *, chunk, tm=1024, tn=1024):
    n, d = a.shape
    n_out = wt.shape[0]
    return pl.pallas_call(
        _matmul_t_kernel,
        grid=(n // tm, n_out // tn),
        in_specs=[
            pl.BlockSpec((tm, d), lambda i, j: (i, 0)),
            pl.BlockSpec((tn, d), lambda i, j: (j, 0)),
        ],
        out_specs=pl.BlockSpec((tm // chunk, tn, chunk), lambda i, j: (i, j, 0)),
        out_shape=jax.ShapeDtypeStruct((n // chunk, n_out, chunk), BF16),
        compiler_params=_params(2),
        name="matmul_t",
    )(a, wt)


def _conv_mix_kernel(b_ref, c_ref, u_ref, cp_ref, up_ref, cn_ref, un_ref,
                     x_ref, cw_ref, wo_ref, o_ref, *, seq):
    i = pl.program_id(0)
    tm = b_ref.shape[0]
    cu = c_ref[...].astype(F32) * u_ref[...].astype(F32)
    last = BF16_SUBLANES - 1
    prev = cp_ref[last:, :].astype(F32) * up_ref[last:, :].astype(F32)
    nxt = cn_ref[:1, :].astype(F32) * un_ref[:1, :].astype(F32)
    prev = jnp.where((i * tm) % seq == 0, 0.0, prev)
    nxt = jnp.where(((i + 1) * tm) % seq == 0, 0.0, nxt)
    row = lax.broadcasted_iota(jnp.int32, cu.shape, 0)
    up = jnp.where(row == 0, prev, pltpu.roll(cu, 1, axis=0))
    dn = jnp.where(row == tm - 1, nxt, pltpu.roll(cu, tm - 1, axis=0))
    cw = cw_ref[...]
    conv = cw[0:1, :] * up + cw[1:2, :] * cu + cw[2:3, :] * dn
    g = (b_ref[...].astype(F32) * conv).astype(BF16)
    o_ref[...] = x_ref[...] + jnp.dot(g, wo_ref[...], preferred_element_type=F32)


def _conv_mix(bcu, x, conv_w, w_out, *, seq, tm=512):
    n, d = x.shape
    hb = BF16_SUBLANES
    tiles_per_halo = tm // hb
    n_halo = n // hb
    prev_idx = lambda i: jnp.maximum(i * tiles_per_halo - 1, 0)
    next_idx = lambda i: jnp.minimum((i + 1) * tiles_per_halo, n_halo - 1)
    return pl.pallas_call(
        functools.partial(_conv_mix_kernel, seq=seq),
        grid=(n // tm,),
        in_specs=[
            pl.BlockSpec((tm, d), lambda i: (i, 0)),
            pl.BlockSpec((tm, d), lambda i: (i, 1)),
            pl.BlockSpec((tm, d), lambda i: (i, 2)),
            pl.BlockSpec((hb, d), lambda i: (prev_idx(i), 1)),
            pl.BlockSpec((hb, d), lambda i: (prev_idx(i), 2)),
            pl.BlockSpec((hb, d), lambda i: (next_idx(i), 1)),
            pl.BlockSpec((hb, d), lambda i: (next_idx(i), 2)),
            pl.BlockSpec((tm, d), lambda i: (i, 0)),
            pl.BlockSpec((3, d), lambda i: (0, 0)),
            pl.BlockSpec((d, d), lambda i: (0, 0)),
        ],
        out_specs=pl.BlockSpec((tm, d), lambda i: (i, 0)),
        out_shape=jax.ShapeDtypeStruct((n, d), F32),
        compiler_params=_params(1),
        name="conv_mix",
    )(bcu, bcu, bcu, bcu, bcu, bcu, bcu, x, conv_w, w_out)


def _mlp_kernel(x_ref, g_ref, w1_ref, w2_ref, gf_ref, o_ref, hn_ref, *, final_norm):
    f = pl.program_id(1)

    @pl.when(f == 0)
    def _():
        _norm_rows_to(hn_ref, x_ref, g_ref, NORM_EPS)
        o_ref[...] = x_ref[...]

    tf = w1_ref.shape[1]
    for c in range(tf // MLP_HIDDEN_GROUP):
        cols = slice(c * MLP_HIDDEN_GROUP, (c + 1) * MLP_HIDDEN_GROUP)
        a = jnp.dot(hn_ref[...], w1_ref[:, cols], preferred_element_type=F32)
        a = jnp.maximum(a, 0.0)
        a = (a * a).astype(BF16)
        o_ref[...] += jnp.dot(a, w2_ref[cols, :], preferred_element_type=F32)

    if final_norm:
        @pl.when(f == pl.num_programs(1) - 1)
        def _():
            _norm_rows_to(o_ref, o_ref, gf_ref, NORM_EPS)


def _mlp(x, g, w1, w2, g_final, *, tm=1024, tf=1024):
    n, d = x.shape
    d_ff = w1.shape[1]
    final_norm = g_final is not None
    gf = (g_final if final_norm else g).reshape(1, d)
    return pl.pallas_call(
        functools.partial(_mlp_kernel, final_norm=final_norm),
        grid=(n // tm, d_ff // tf),
        in_specs=[
            pl.BlockSpec((tm, d), lambda i, f: (i, 0)),
            pl.BlockSpec((1, d), lambda i, f: (0, 0)),
            pl.BlockSpec((d, tf), lambda i, f: (0, f)),
            pl.BlockSpec((tf, d), lambda i, f: (f, 0)),
            pl.BlockSpec((1, d), lambda i, f: (0, 0)),
        ],
        out_specs=pl.BlockSpec((tm, d), lambda i, f: (i, 0)),
        out_shape=jax.ShapeDtypeStruct((n, d), F32),
        scratch_shapes=[pltpu.VMEM((tm, d), BF16)],
        compiler_params=_params(2),
        name="mlp",
    )(x, g.reshape(1, d), w1, w2, gf)


def _attn_tile_ids(t, batch, q_tiles):
    return t // (batch * q_tiles), t % q_tiles, (t // q_tiles) % batch


def _attn_kernel(lq1_ref, lk1_ref, lq2_ref, lk2_ref, subln_ref, q_ref, k_ref, vt_ref,
                 o_ref, rel_ref, t0_ref, t1_ref, m0_ref, m1_ref, acc0_ref, acc1_ref,
                 *, batch, q_tiles, lambda_init):
    n = pl.program_id(0)
    n_tiles = pl.num_programs(0) - ATTN_PIPELINE_DRAIN_STEPS
    h, qi, b = _attn_tile_ids(jnp.minimum(n, n_tiles - 1), batch, q_tiles)
    tq = q_ref.shape[0]
    seq = k_ref.shape[0]
    n_kc, _, kc = vt_ref.shape
    w = 2 * tq

    @pl.when(n == 0)
    def _():
        t1_ref[...] = jnp.zeros_like(t1_ref)
        m1_ref[...] = jnp.zeros_like(m1_ref)
        acc0_ref[...] = jnp.ones_like(acc0_ref)
        acc1_ref[...] = jnp.ones_like(acc1_ref)

    @pl.when(jnp.logical_and(qi == 0, b == 0))
    def _():
        slope = jnp.exp2(jnp.full((1, 1), -8.0 / N_HEADS, F32) * (h + 1).astype(F32))
        neg_slope = -slope * LOG2E

        def body(c, carry):
            u = c * tq + lax.broadcasted_iota(jnp.int32, (tq, tq), 0)
            col = lax.broadcasted_iota(jnp.int32, (tq, tq), 1)
            rows = pl.ds(pl.multiple_of(c * tq, tq), tq)
            rel_ref[rows, :] = neg_slope * jnp.abs(col - u + (seq - tq)).astype(F32)
            return carry

        lax.fori_loop(0, rel_ref.shape[0] // tq, body, 0)

    bias_base = (seq - tq) - qi * tq

    def step(t_cur, m_cur, acc_cur, t_prev, m_prev, acc_prev):
        acc = acc_prev[...]
        a0, a1 = acc[:HEAD_DIM, :tq], acc[:HEAD_DIM, tq:]
        l0, l1 = acc[HEAD_DIM:HEAD_DIM + 1, :tq], acc[HEAD_DIM:HEAD_DIM + 1, tq:]
        lam = (jnp.exp(jnp.sum(lq1_ref[...] * lk1_ref[...], keepdims=True))
               - jnp.exp(jnp.sum(lq2_ref[...] * lk2_ref[...], keepdims=True)) + lambda_init)
        o_t = a0 / l0 - lam * (a1 / l1)
        ms = jnp.mean(o_t * o_t, axis=0, keepdims=True)
        o_t = o_t * lax.rsqrt(ms + SUBLN_EPS) * subln_ref[...] * (1.0 - lambda_init)
        o_ref[...] = o_t.T.astype(o_ref.dtype)

        q = q_ref[...]
        lane = lax.broadcasted_iota(jnp.int32, q.shape, 1)
        zero = jnp.zeros_like(q)
        q_both = jnp.concatenate([jnp.where(lane < HALF_DIM, q, zero),
                                  jnp.where(lane >= HALF_DIM, q, zero)], axis=0)
        m_run = jnp.full((SUBLANES, w), -1e30, F32)

        def pass1_chunk(j, m_run):
            s = lax.dot_general(k_ref[j * kc:(j + 1) * kc, :], q_both, NT_DIMS,
                                preferred_element_type=F32)
            for r in range(kc // SUBLANES):
                rows = slice(j * kc + r * SUBLANES, j * kc + (r + 1) * SUBLANES)
                rel_rows = pl.ds(pl.multiple_of(bias_base + rows.start, SUBLANES), SUBLANES)
                br = rel_ref[rel_rows, :]
                t = s[r * SUBLANES:(r + 1) * SUBLANES, :] + jnp.concatenate([br, br], axis=1)
                t_cur[rows, :] = t
                m_run = jnp.maximum(m_run, t)
            return m_run

        ones_rows = (lax.broadcasted_iota(jnp.int32, (BF16_SUBLANES, kc), 0) == 0).astype(BF16)
        m = m_prev[...]
        slab = BF16_SUBLANES

        def pass2_chunk(j):
            parts = []
            for r in range(kc // slab):
                rows = slice(j * kc + r * slab, j * kc + (r + 1) * slab)
                x = t_prev[rows, :].reshape(slab // SUBLANES, SUBLANES, w) - m[None]
                parts.append(jnp.exp2(x.reshape(slab, w).astype(BF16)))
            e = jnp.concatenate(parts, axis=0)
            v_aug = jnp.concatenate([vt_ref[j], ones_rows], axis=0)
            return jnp.dot(v_aug, e, preferred_element_type=F32)

        acc = None
        for j in range(n_kc):
            m_run = pass1_chunk(j, m_run)
            pv = pass2_chunk(j)
            acc = pv if acc is None else acc + pv
        m_cur[...] = jnp.broadcast_to(jnp.max(m_run, axis=0, keepdims=True), (SUBLANES, w))
        acc_cur[...] = acc

    @pl.when(n % 2 == 0)
    def _():
        step(t0_ref, m0_ref, acc0_ref, t1_ref, m1_ref, acc1_ref)

    @pl.when(n % 2 == 1)
    def _():
        step(t1_ref, m1_ref, acc1_ref, t0_ref, m0_ref, acc0_ref)


def _attention(qk, vt, lq1, lk1, lq2, lk2, subln, *, batch, seq, lambda_init, tq=256):
    n = qk.shape[0]
    d = N_HEADS * HEAD_DIM
    kc = vt.shape[2]
    chunks = seq // kc
    q_tiles = seq // tq
    n_tiles = N_HEADS * q_tiles * batch
    tile = lambda s, lag: _attn_tile_ids(jnp.clip(s - lag, 0, n_tiles - 1), batch, q_tiles)

    def q_map(s):
        h, qi, b = tile(s, 0)
        return b * q_tiles + qi, h

    def k_map(s):
        h, _, b = tile(s, 0)
        return b, N_HEADS + h

    def vt_map(s):
        h, _, b = tile(s, 1)
        return b, h, 0

    def o_map(s):
        h, qi, b = tile(s, 2)
        return b * q_tiles + qi, h

    vec = lambda a: a.reshape(1, -1).astype(F32)
    small = lambda width: pl.BlockSpec((1, width), lambda s: (0, 0))
    return pl.pallas_call(
        functools.partial(_attn_kernel, batch=batch, q_tiles=q_tiles, lambda_init=lambda_init),
        grid=(n_tiles + ATTN_PIPELINE_DRAIN_STEPS,),
        in_specs=[
            small(HALF_DIM), small(HALF_DIM), small(HALF_DIM), small(HALF_DIM),
            pl.BlockSpec((HEAD_DIM, 1), lambda s: (0, 0)),
            pl.BlockSpec((tq, HEAD_DIM), q_map),
            pl.BlockSpec((seq, HEAD_DIM), k_map),
            pl.BlockSpec((chunks, HEAD_DIM, kc), vt_map),
        ],
        out_specs=pl.BlockSpec((tq, HEAD_DIM), o_map),
        out_shape=jax.ShapeDtypeStruct((n, d), BF16),
        scratch_shapes=[
            pltpu.VMEM((2 * seq - tq, tq), F32),
            pltpu.VMEM((seq, 2 * tq), F32),
            pltpu.VMEM((seq, 2 * tq), F32),
            pltpu.VMEM((SUBLANES, 2 * tq), F32),
            pltpu.VMEM((SUBLANES, 2 * tq), F32),
            pltpu.VMEM((HEAD_DIM + BF16_SUBLANES, 2 * tq), F32),
            pltpu.VMEM((HEAD_DIM + BF16_SUBLANES, 2 * tq), F32),
        ],
        compiler_params=_params(1),
        name="diff_attention",
    )(vec(lq1), vec(lk1), vec(lq2), vec(lk2), subln.reshape(HEAD_DIM, 1).astype(F32), qk, qk, vt)


def _proj_residual_kernel(a_ref, x_ref, w_ref, o_ref):
    o_ref[...] = x_ref[...] + jnp.dot(a_ref[...], w_ref[...], preferred_element_type=F32)


def _proj_residual(a, x, w, *, tm=512):
    n, d = x.shape
    return pl.pallas_call(
        _proj_residual_kernel,
        grid=(n // tm,),
        in_specs=[
            pl.BlockSpec((tm, d), lambda i: (i, 0)),
            pl.BlockSpec((tm, d), lambda i: (i, 0)),
            pl.BlockSpec((d, d), lambda i: (0, 0)),
        ],
        out_specs=pl.BlockSpec((tm, d), lambda i: (i, 0)),
        out_shape=jax.ShapeDtypeStruct((n, d), F32),
        compiler_params=_params(1),
        name="proj_residual",
    )(a, x, w)


def kernel(x, ln_mix, ln_mlp, conv_w_in, conv_w, conv_w_out, attn_w_qkv, attn_lambda_q1,
           attn_lambda_k1, attn_lambda_q2, attn_lambda_k2, attn_subln, attn_w_o, mlp_w1,
           mlp_w2, ln_f):
    batch, seq, d = x.shape
    depth = ln_mix.shape[0]
    h = x.reshape(batch * seq, d)
    for i in range(depth):
        j = i // N_MIXERS
        if i % N_MIXERS == 0:
            bcu = _norm_matmul(h, ln_mix[i], conv_w_in[j].astype(BF16))
            h = _conv_mix(bcu, h, conv_w[j], conv_w_out[j].astype(BF16), seq=seq)
        else:
            w_qkv = attn_w_qkv[j]
            qk, hn = _norm_matmul(h, ln_mix[i], w_qkv[:, :2 * d].astype(BF16), emit_normed=True,
                                  scaled_cols=d, scale=ATTN_SCALE * LOG2E)
            vt = _matmul_t(hn, w_qkv[:, 2 * d:].T.astype(BF16), chunk=ATTN_KEY_CHUNK)
            o = _attention(qk, vt, attn_lambda_q1[j], attn_lambda_k1[j], attn_lambda_q2[j],
                           attn_lambda_k2[j], attn_subln[j], batch=batch, seq=seq,
                           lambda_init=_lambda_init(i))
            h = _proj_residual(o, h, attn_w_o[j].astype(BF16))
        g_final = ln_f if i == depth - 1 else None
        h = _mlp(h, ln_mlp[i], mlp_w1[i].astype(BF16), mlp_w2[i].astype(BF16), g_final)
    return h.reshape(batch, seq, d)
```

```python
import functools
import math

import jax
import jax.numpy as jnp
from jax import lax
from jax.experimental import pallas as pl
from jax.experimental.pallas import tpu as pltpu

N_HEADS = 16
HEAD_DIM = 128
HALF_DIM = HEAD_DIM // 2
ATTN_SCALE = HALF_DIM ** -0.5
LOG2E = math.log2(math.e)
NORM_EPS = 1e-6
SUBLN_EPS = 1e-5
N_MIXERS = 2

V7X_VMEM_BYTES = 64 * 1024 * 1024
VMEM_LIMIT_BYTES = V7X_VMEM_BYTES - 8 * 1024 * 1024
LANES = 128
SUBLANES = 8
BF16_SUBLANES = 16
NORM_ROWS = 256
MLP_HIDDEN_GROUP = 512
ATTN_KEY_CHUNK = 512
ATTN_PIPELINE_DRAIN_STEPS = 2

F32 = jnp.float32
BF16 = jnp.bfloat16
NT_DIMS = (((1,), (1,)), ((), ()))


def _lambda_init(layer_idx):
    return 0.8 - 0.6 * math.exp(-0.3 * layer_idx)


def _rmsnorm(x, g, eps):
    ms = jnp.mean(x * x, axis=-1, keepdims=True)
    return x * lax.rsqrt(ms + eps) * g


def _norm_rows_to(dst_ref, src_ref, g_ref, eps):
    rows = src_ref.shape[0]
    g = g_ref[...]

    def body(c, carry):
        sl = pl.ds(pl.multiple_of(c * NORM_ROWS, NORM_ROWS), NORM_ROWS)
        dst_ref[sl, :] = _rmsnorm(src_ref[sl, :], g, eps).astype(dst_ref.dtype)
        return carry

    lax.fori_loop(0, rows // NORM_ROWS, body, 0)


def _params(n_axes):
    return pltpu.CompilerParams(
        dimension_semantics=("arbitrary",) * n_axes,
        vmem_limit_bytes=VMEM_LIMIT_BYTES)


def _norm_matmul_kernel(x_ref, g_ref, w_ref, o_ref, hn_ref, *, scaled_tiles, scale):
    j = pl.program_id(1)

    @pl.when(j == 0)
    def _():
        _norm_rows_to(hn_ref, x_ref, g_ref, NORM_EPS)

    acc = jnp.dot(hn_ref[...], w_ref[...], preferred_element_type=F32)
    if scaled_tiles:
        acc = acc * jnp.where(j < scaled_tiles, scale, 1.0).astype(F32)
    o_ref[...] = acc.astype(o_ref.dtype)


def _norm_matmul(x, g, w, *, emit_normed=False, scaled_cols=0, scale=1.0, tm=1024, tn=2048):
    n, d = x.shape
    n_out = w.shape[1]
    out_specs = [pl.BlockSpec((tm, tn), lambda i, j: (i, j))]
    out_shape = [jax.ShapeDtypeStruct((n, n_out), BF16)]
    scratch_shapes = []
    if emit_normed:
        out_specs.append(pl.BlockSpec((tm, d), lambda i, j: (i, 0)))
        out_shape.append(jax.ShapeDtypeStruct((n, d), BF16))
    else:
        scratch_shapes.append(pltpu.VMEM((tm, d), BF16))
    outs = pl.pallas_call(
        functools.partial(_norm_matmul_kernel, scaled_tiles=scaled_cols // tn, scale=scale),
        grid=(n // tm, n_out // tn),
        in_specs=[
            pl.BlockSpec((tm, d), lambda i, j: (i, 0)),
            pl.BlockSpec((1, d), lambda i, j: (0, 0)),
            pl.BlockSpec((d, tn), lambda i, j: (0, j)),
        ],
        out_specs=out_specs,
        out_shape=out_shape,
        scratch_shapes=scratch_shapes,
        compiler_params=_params(2),
        name="norm_matmul",
    )(x, g.reshape(1, d), w)
    return outs if emit_normed else outs[0]


def _matmul_t_kernel(a_ref, wt_ref, o_ref):
    chunk = o_ref.shape[2]
    out = lax.dot_general(wt_ref[...], a_ref[...], NT_DIMS,
                          preferred_element_type=F32).astype(o_ref.dtype)
    for c in range(o_ref.shape[0]):
        o_ref[c] = out[:, c * chunk:(c + 1) * chunk]


def _matmul_t(a, wt, *, chunk, tm=1024, tn=1024):
    n, d = a.shape
    n_out = wt.shape[0]
    return pl.pallas_call(
        _matmul_t_kernel,
        grid=(n // tm, n_out // tn),
        in_specs=[
            pl.BlockSpec((tm, d), lambda i, j: (i, 0)),
            pl.BlockSpec((tn, d), lambda i, j: (j, 0)),
        ],
        out_specs=pl.BlockSpec((tm // chunk, tn, chunk), lambda i, j: (i, j, 0)),
        out_shape=jax.ShapeDtypeStruct((n // chunk, n_out, chunk), BF16),
        compiler_params=_params(2),
        name="matmul_t",
    )(a, wt)


def _conv_mix_kernel(b_ref, c_ref, u_ref, cp_ref, up_ref, cn_ref, un_ref,
                     x_ref, cw_ref, wo_ref, o_ref, *, seq):
    i = pl.program_id(0)
    tm = b_ref.shape[0]
    cu = c_ref[...].astype(F32) * u_ref[...].astype(F32)
    last = BF16_SUBLANES - 1
    prev = cp_ref[last:, :].astype(F32) * up_ref[last:, :].astype(F32)
    nxt = cn_ref[:1, :].astype(F32) * un_ref[:1, :].astype(F32)
    prev = jnp.where((i * tm) % seq == 0, 0.0, prev)
    nxt = jnp.where(((i + 1) * tm) % seq == 0, 0.0, nxt)
    row = lax.broadcasted_iota(jnp.int32, cu.shape, 0)
    up = jnp.where(row == 0, prev, pltpu.roll(cu, 1, axis=0))
    dn = jnp.where(row == tm - 1, nxt, pltpu.roll(cu, tm - 1, axis=0))
    cw = cw_ref[...]
    conv = cw[0:1, :] * up + cw[1:2, :] * cu + cw[2:3, :] * dn
    g = (b_ref[...].astype(F32) * conv).astype(BF16)
    o_ref[...] = x_ref[...] + jnp.dot(g, wo_ref[...], preferred_element_type=F32)


def _conv_mix(bcu, x, conv_w, w_out, *, seq, tm=512):
    n, d = x.shape
    hb = BF16_SUBLANES
    tiles_per_halo = tm // hb
    n_halo = n // hb
    prev_idx = lambda i: jnp.maximum(i * tiles_per_halo - 1, 0)
    next_idx = lambda i: jnp.minimum((i + 1) * tiles_per_halo, n_halo - 1)
    return pl.pallas_call(
        functools.partial(_conv_mix_kernel, seq=seq),
        grid=(n // tm,),
        in_specs=[
            pl.BlockSpec((tm, d), lambda i: (i, 0)),
            pl.BlockSpec((tm, d), lambda i: (i, 1)),
            pl.BlockSpec((tm, d), lambda i: (i, 2)),
            pl.BlockSpec((hb, d), lambda i: (prev_idx(i), 1)),
            pl.BlockSpec((hb, d), lambda i: (prev_idx(i), 2)),
            pl.BlockSpec((hb, d), lambda i: (next_idx(i), 1)),
            pl.BlockSpec((hb, d), lambda i: (next_idx(i), 2)),
            pl.BlockSpec((tm, d), lambda i: (i, 0)),
            pl.BlockSpec((3, d), lambda i: (0, 0)),
            pl.BlockSpec((d, d), lambda i: (0, 0)),
        ],
        out_specs=pl.BlockSpec((tm, d), lambda i: (i, 0)),
        out_shape=jax.ShapeDtypeStruct((n, d), F32),
        compiler_params=_params(1),
        name="conv_mix",
    )(bcu, bcu, bcu, bcu, bcu, bcu, bcu, x, conv_w, w_out)


def _mlp_kernel(x_ref, g_ref, w1_ref, w2_ref, gf_ref, o_ref, hn_ref, *, final_norm):
    f = pl.program_id(1)

    @pl.when(f == 0)
    def _():
        _norm_rows_to(hn_ref, x_ref, g_ref, NORM_EPS)
        o_ref[...] = x_ref[...]

    tf = w1_ref.shape[1]
    for c in range(tf // MLP_HIDDEN_GROUP):
        cols = slice(c * MLP_HIDDEN_GROUP, (c + 1) * MLP_HIDDEN_GROUP)
        a = jnp.dot(hn_ref[...], w1_ref[:, cols], preferred_element_type=F32)
        a = jnp.maximum(a, 0.0)
        a = (a * a).astype(BF16)
        o_ref[...] += jnp.dot(a, w2_ref[cols, :], preferred_element_type=F32)

    if final_norm:
        @pl.when(f == pl.num_programs(1) - 1)
        def _():
            _norm_rows_to(o_ref, o_ref, gf_ref, NORM_EPS)


def _mlp(x, g, w1, w2, g_final, *, tm=1024, tf=1024):
    n, d = x.shape
    d_ff = w1.shape[1]
    final_norm = g_final is not None
    gf = (g_final if final_norm else g).reshape(1, d)
    return pl.pallas_call(
        functools.partial(_mlp_kernel, final_norm=final_norm),
        grid=(n // tm, d_ff // tf),
        in_specs=[
            pl.BlockSpec((tm, d), lambda i, f: (i, 0)),
            pl.BlockSpec((1, d), lambda i, f: (0, 0)),
            pl.BlockSpec((d, tf), lambda i, f: (0, f)),
            pl.BlockSpec((tf, d), lambda i, f: (f, 0)),
            pl.BlockSpec((1, d), lambda i, f: (0, 0)),
        ],
        out_specs=pl.BlockSpec((tm, d), lambda i, f: (i, 0)),
        out_shape=jax.ShapeDtypeStruct((n, d), F32),
        scratch_shapes=[pltpu.VMEM((tm, d), BF16)],
        compiler_params=_params(2),
        name="mlp",
    )(x, g.reshape(1, d), w1, w2, gf)


def _attn_tile_ids(t, batch, q_tiles):
    return t // (batch * q_tiles), t % q_tiles, (t // q_tiles) % batch


def _attn_kernel(lq1_ref, lk1_ref, lq2_ref, lk2_ref, subln_ref, q_ref, k_ref, vt_ref,
                 o_ref, rel_ref, t0_ref, t1_ref, m0_ref, m1_ref, acc0_ref, acc1_ref,
                 *, batch, q_tiles, lambda_init):
    n = pl.program_id(0)
    n_tiles = pl.num_programs(0) - ATTN_PIPELINE_DRAIN_STEPS
    h, qi, b = _attn_tile_ids(jnp.minimum(n, n_tiles - 1), batch, q_tiles)
    n_sub, seq, w = t0_ref.shape
    tq = w // 2
    n_kc, _, kc = vt_ref.shape

    @pl.when(n == 0)
    def _():
        t1_ref[...] = jnp.zeros_like(t1_ref)
        m1_ref[...] = jnp.zeros_like(m1_ref)
        acc0_ref[...] = jnp.ones_like(acc0_ref)
        acc1_ref[...] = jnp.ones_like(acc1_ref)

    @pl.when(jnp.logical_and(qi == 0, b == 0))
    def _():
        slope = jnp.exp2(jnp.full((1, 1), -8.0 / N_HEADS, F32) * (h + 1).astype(F32))
        neg_slope = -slope * LOG2E

        def body(c, carry):
            u = c * LANES + lax.broadcasted_iota(jnp.int32, (LANES, LANES), 0)
            col = lax.broadcasted_iota(jnp.int32, (LANES, LANES), 1)
            rows = pl.ds(pl.multiple_of(c * LANES, LANES), LANES)
            rel_ref[rows, :] = neg_slope * jnp.abs(col - u + (seq - LANES)).astype(F32)
            return carry

        lax.fori_loop(0, rel_ref.shape[0] // LANES, body, 0)

    def sub_tile(u, t_cur, m_cur, acc_cur, t_prev, m_prev, acc_prev):
        q_rows = slice(u * tq, (u + 1) * tq)
        bias_base = (seq - LANES) - (qi * n_sub + u) * tq

        def bias_rows(key_start):
            groups = [rel_ref[pl.ds(pl.multiple_of(bias_base + key_start - g * LANES, SUBLANES),
                                    SUBLANES), :] for g in range(tq // LANES)]
            return jnp.concatenate(groups + groups, axis=1)

        acc = acc_prev[u]
        a0, a1 = acc[:HEAD_DIM, :tq], acc[:HEAD_DIM, tq:]
        l0, l1 = acc[HEAD_DIM:HEAD_DIM + 1, :tq], acc[HEAD_DIM:HEAD_DIM + 1, tq:]
        lam = (jnp.exp(jnp.sum(lq1_ref[...] * lk1_ref[...], keepdims=True))
               - jnp.exp(jnp.sum(lq2_ref[...] * lk2_ref[...], keepdims=True)) + lambda_init)
        o_t = a0 / l0 - lam * (a1 / l1)
        ms = jnp.mean(o_t * o_t, axis=0, keepdims=True)
        o_t = o_t * lax.rsqrt(ms + SUBLN_EPS) * subln_ref[...] * (1.0 - lambda_init)
        o_ref[q_rows, :] = o_t.T.astype(o_ref.dtype)

        q = q_ref[q_rows, :]
        lane = lax.broadcasted_iota(jnp.int32, q.shape, 1)
        zero = jnp.zeros_like(q)
        q_both = jnp.concatenate([jnp.where(lane < HALF_DIM, q, zero),
                                  jnp.where(lane >= HALF_DIM, q, zero)], axis=0)
        m_run = jnp.full((SUBLANES, w), -1e30, F32)

        def pass1_chunk(j, m_run):
            s = lax.dot_general(k_ref[j * kc:(j + 1) * kc, :], q_both, NT_DIMS,
                                preferred_element_type=F32)
            for r in range(kc // SUBLANES):
                rows = slice(j * kc + r * SUBLANES, j * kc + (r + 1) * SUBLANES)
                t = s[r * SUBLANES:(r + 1) * SUBLANES, :] + bias_rows(rows.start)
                t_cur[u, rows, :] = t
                m_run = jnp.maximum(m_run, t)
            return m_run

        ones_rows = (lax.broadcasted_iota(jnp.int32, (BF16_SUBLANES, kc), 0) == 0).astype(BF16)
        m = m_prev[u]
        slab = BF16_SUBLANES

        def pass2_chunk(j):
            parts = []
            for r in range(kc // slab):
                rows = slice(j * kc + r * slab, j * kc + (r + 1) * slab)
                x = t_prev[u, rows, :].reshape(slab // SUBLANES, SUBLANES, w) - m[None]
                parts.append(jnp.exp2(x.reshape(slab, w).astype(BF16)))
            e = jnp.concatenate(parts, axis=0)
            v_aug = jnp.concatenate([vt_ref[j], ones_rows], axis=0)
            return jnp.dot(v_aug, e, preferred_element_type=F32)

        acc = None
        for j in range(n_kc):
            m_run = pass1_chunk(j, m_run)
            pv = pass2_chunk(j)
            acc = pv if acc is None else acc + pv
        m_cur[u] = jnp.broadcast_to(jnp.max(m_run, axis=0, keepdims=True), (SUBLANES, w))
        acc_cur[u] = acc

    def step(*bufs):
        for u in range(n_sub):
            sub_tile(u, *bufs)

    @pl.when(n % 2 == 0)
    def _():
        step(t0_ref, m0_ref, acc0_ref, t1_ref, m1_ref, acc1_ref)

    @pl.when(n % 2 == 1)
    def _():
        step(t1_ref, m1_ref, acc1_ref, t0_ref, m0_ref, acc0_ref)


def _attention(qk, vt, lq1, lk1, lq2, lk2, subln, *, batch, seq, lambda_init, tq=256, n_sub=2):
    n = qk.shape[0]
    d = N_HEADS * HEAD_DIM
    kc = vt.shape[2]
    chunks = seq // kc
    rows = n_sub * tq
    q_tiles = seq // rows
    n_tiles = N_HEADS * q_tiles * batch
    acc_rows = HEAD_DIM + BF16_SUBLANES
    tile = lambda s, lag: _attn_tile_ids(jnp.clip(s - lag, 0, n_tiles - 1), batch, q_tiles)

    def q_map(s):
        h, qi, b = tile(s, 0)
        return b * q_tiles + qi, h

    def k_map(s):
        h, _, b = tile(s, 0)
        return b, N_HEADS + h

    def vt_map(s):
        h, _, b = tile(s, 1)
        return b, h, 0

    def o_map(s):
        h, qi, b = tile(s, 2)
        return b * q_tiles + qi, h

    vec = lambda a: a.reshape(1, -1).astype(F32)
    small = lambda width: pl.BlockSpec((1, width), lambda s: (0, 0))
    return pl.pallas_call(
        functools.partial(_attn_kernel, batch=batch, q_tiles=q_tiles, lambda_init=lambda_init),
        grid=(n_tiles + ATTN_PIPELINE_DRAIN_STEPS,),
        in_specs=[
            small(HALF_DIM), small(HALF_DIM), small(HALF_DIM), small(HALF_DIM),
            pl.BlockSpec((HEAD_DIM, 1), lambda s: (0, 0)),
            pl.BlockSpec((rows, HEAD_DIM), q_map),
            pl.BlockSpec((seq, HEAD_DIM), k_map),
            pl.BlockSpec((chunks, HEAD_DIM, kc), vt_map),
        ],
        out_specs=pl.BlockSpec((rows, HEAD_DIM), o_map),
        out_shape=jax.ShapeDtypeStruct((n, d), BF16),
        scratch_shapes=[
            pltpu.VMEM((2 * seq - LANES, LANES), F32),
            pltpu.VMEM((n_sub, seq, 2 * tq), F32),
            pltpu.VMEM((n_sub, seq, 2 * tq), F32),
            pltpu.VMEM((n_sub, SUBLANES, 2 * tq), F32),
            pltpu.VMEM((n_sub, SUBLANES, 2 * tq), F32),
            pltpu.VMEM((n_sub, acc_rows, 2 * tq), F32),
            pltpu.VMEM((n_sub, acc_rows, 2 * tq), F32),
        ],
        compiler_params=_params(1),
        name="diff_attention",
    )(vec(lq1), vec(lk1), vec(lq2), vec(lk2), subln.reshape(HEAD_DIM, 1).astype(F32), qk, qk, vt)


def _proj_residual_kernel(a_ref, x_ref, w_ref, o_ref):
    o_ref[...] = x_ref[...] + jnp.dot(a_ref[...], w_ref[...], preferred_element_type=F32)


def _proj_residual(a, x, w, *, tm=512):
    n, d = x.shape
    return pl.pallas_call(
        _proj_residual_kernel,
        grid=(n // tm,),
        in_specs=[
            pl.BlockSpec((tm, d), lambda i: (i, 0)),
            pl.BlockSpec((tm, d), lambda i: (i, 0)),
            pl.BlockSpec((d, d), lambda i: (0, 0)),
        ],
        out_specs=pl.BlockSpec((tm, d), lambda i: (i, 0)),
        out_shape=jax.ShapeDtypeStruct((n, d), F32),
        compiler_params=_params(1),
        name="proj_residual",
    )(a, x, w)


def kernel(x, ln_mix, ln_mlp, conv_w_in, conv_w, conv_w_out, attn_w_qkv, attn_lambda_q1,
           attn_lambda_k1, attn_lambda_q2, attn_lambda_k2, attn_subln, attn_w_o, mlp_w1,
           mlp_w2, ln_f):
    batch, seq, d = x.shape
    depth = ln_mix.shape[0]
    h = x.reshape(batch * seq, d)
    for i in range(depth):
        j = i // N_MIXERS
        if i % N_MIXERS == 0:
            bcu = _norm_matmul(h, ln_mix[i], conv_w_in[j].astype(BF16))
            h = _conv_mix(bcu, h, conv_w[j], conv_w_out[j].astype(BF16), seq=seq)
        else:
            w_qkv = attn_w_qkv[j]
            qk, hn = _norm_matmul(h, ln_mix[i], w_qkv[:, :2 * d].astype(BF16), emit_normed=True,
                                  scaled_cols=d, scale=ATTN_SCALE * LOG2E)
            vt = _matmul_t(hn, w_qkv[:, 2 * d:].T.astype(BF16), chunk=ATTN_KEY_CHUNK)
            o = _attention(qk, vt, attn_lambda_q1[j], attn_lambda_k1[j], attn_lambda_q2[j],
                           attn_lambda_k2[j], attn_subln[j], batch=batch, seq=seq,
                           lambda_init=_lambda_init(i))
            h = _proj_residual(o, h, attn_w_o[j].astype(BF16))
        g_final = ln_f if i == depth - 1 else None
        h = _mlp(h, ln_mlp[i], mlp_w1[i].astype(BF16), mlp_w2[i].astype(BF16), g_final)
    return h.reshape(batch, seq, d)
```

```python
import functools
import math

import jax
import jax.numpy as jnp
from jax import lax
from jax.experimental import pallas as pl
from jax.experimental.pallas import tpu as pltpu

N_HEADS = 16
HEAD_DIM = 128
HALF_DIM = HEAD_DIM // 2
ATTN_SCALE = HALF_DIM ** -0.5
LOG2E = math.log2(math.e)
NORM_EPS = 1e-6
SUBLN_EPS = 1e-5
N_MIXERS = 2

V7X_VMEM_BYTES = 64 * 1024 * 1024
VMEM_LIMIT_BYTES = V7X_VMEM_BYTES - 8 * 1024 * 1024
LANES = 128
SUBLANES = 8
BF16_SUBLANES = 16
NORM_ROWS = 256
MLP_HIDDEN_GROUP = 512
CAST_BLOCK_ELEMS = 2 * 1024 * 1024
ATTN_KEY_CHUNK = 512
ATTN_PIPELINE_DRAIN_STEPS = 2

F32 = jnp.float32
BF16 = jnp.bfloat16
NT_DIMS = (((1,), (1,)), ((), ()))


def _lambda_init(layer_idx):
    return 0.8 - 0.6 * math.exp(-0.3 * layer_idx)


def _rmsnorm(x, g, eps):
    ms = jnp.mean(x * x, axis=-1, keepdims=True)
    return x * lax.rsqrt(ms + eps) * g


def _norm_rows_to(dst_ref, src_ref, g_ref, eps):
    rows = src_ref.shape[0]
    g = g_ref[...]

    def body(c, carry):
        sl = pl.ds(pl.multiple_of(c * NORM_ROWS, NORM_ROWS), NORM_ROWS)
        dst_ref[sl, :] = _rmsnorm(src_ref[sl, :], g, eps).astype(dst_ref.dtype)
        return carry

    lax.fori_loop(0, rows // NORM_ROWS, body, 0)


def _params(n_axes):
    return pltpu.CompilerParams(
        dimension_semantics=("arbitrary",) * n_axes,
        vmem_limit_bytes=VMEM_LIMIT_BYTES)


def _cast_kernel(w_ref, o_ref):
    o_ref[...] = w_ref[...].astype(o_ref.dtype)


def _weight_bf16(w, layer, *, cols=None):
    _, r, c = w.shape
    c = c if cols is None else cols
    tr = min(r, 1 << ((CAST_BLOCK_ELEMS // c).bit_length() - 1))
    return pl.pallas_call(
        _cast_kernel,
        grid=(r // tr,),
        in_specs=[pl.BlockSpec((None, tr, c), lambda i: (layer, i, 0))],
        out_specs=pl.BlockSpec((tr, c), lambda i: (i, 0)),
        out_shape=jax.ShapeDtypeStruct((r, c), BF16),
        compiler_params=_params(1),
        name="weight_bf16",
    )(w)


def _norm_matmul_kernel(x_ref, g_ref, w_ref, o_ref, hn_ref, *, scaled_tiles, scale):
    j = pl.program_id(1)

    @pl.when(j == 0)
    def _():
        _norm_rows_to(hn_ref, x_ref, g_ref, NORM_EPS)

    acc = jnp.dot(hn_ref[...], w_ref[...], preferred_element_type=F32)
    if scaled_tiles:
        acc = acc * jnp.where(j < scaled_tiles, scale, 1.0).astype(F32)
    o_ref[...] = acc.astype(o_ref.dtype)


def _norm_matmul(x, g, w, *, emit_normed=False, scaled_cols=0, scale=1.0, tm=1024, tn=2048):
    n, d = x.shape
    n_out = w.shape[1]
    out_specs = [pl.BlockSpec((tm, tn), lambda i, j: (i, j))]
    out_shape = [jax.ShapeDtypeStruct((n, n_out), BF16)]
    scratch_shapes = []
    if emit_normed:
        out_specs.append(pl.BlockSpec((tm, d), lambda i, j: (i, 0)))
        out_shape.append(jax.ShapeDtypeStruct((n, d), BF16))
    else:
        scratch_shapes.append(pltpu.VMEM((tm, d), BF16))
    outs = pl.pallas_call(
        functools.partial(_norm_matmul_kernel, scaled_tiles=scaled_cols // tn, scale=scale),
        grid=(n // tm, n_out // tn),
        in_specs=[
            pl.BlockSpec((tm, d), lambda i, j: (i, 0)),
            pl.BlockSpec((1, d), lambda i, j: (0, 0)),
            pl.BlockSpec((d, tn), lambda i, j: (0, j)),
        ],
        out_specs=out_specs,
        out_shape=out_shape,
        scratch_shapes=scratch_shapes,
        compiler_params=_params(2),
        name="norm_matmul",
    )(x, g.reshape(1, d), w)
    return outs if emit_normed else outs[0]


def _matmul_t_kernel(a_ref, wt_ref, o_ref):
    chunk = o_ref.shape[2]
    out = lax.dot_general(wt_ref[...], a_ref[...], NT_DIMS,
                          preferred_element_type=F32).astype(o_ref.dtype)
    for c in range(o_ref.shape[0]):
        o_ref[c] = out[:, c * chunk:(c + 1) * chunk]


def _matmul_t(a, wt, *, chunk, tm=1024, tn=1024):
    n, d = a.shape
    n_out = wt.shape[0]
    return pl.pallas_call(
        _matmul_t_kernel,
        grid=(n // tm, n_out // tn),
        in_specs=[
            pl.BlockSpec((tm, d), lambda i, j: (i, 0)),
            pl.BlockSpec((tn, d), lambda i, j: (j, 0)),
        ],
        out_specs=pl.BlockSpec((tm // chunk, tn, chunk), lambda i, j: (i, j, 0)),
        out_shape=jax.ShapeDtypeStruct((n // chunk, n_out, chunk), BF16),
        compiler_params=_params(2),
        name="matmul_t",
    )(a, wt)


def _conv_mix_kernel(b_ref, c_ref, u_ref, cp_ref, up_ref, cn_ref, un_ref,
                     x_ref, cw_ref, wo_ref, o_ref, *, seq):
    i = pl.program_id(0)
    tm = b_ref.shape[0]
    cu = c_ref[...].astype(F32) * u_ref[...].astype(F32)
    last = BF16_SUBLANES - 1
    prev = cp_ref[last:, :].astype(F32) * up_ref[last:, :].astype(F32)
    nxt = cn_ref[:1, :].astype(F32) * un_ref[:1, :].astype(F32)
    prev = jnp.where((i * tm) % seq == 0, 0.0, prev)
    nxt = jnp.where(((i + 1) * tm) % seq == 0, 0.0, nxt)
    row = lax.broadcasted_iota(jnp.int32, cu.shape, 0)
    up = jnp.where(row == 0, prev, pltpu.roll(cu, 1, axis=0))
    dn = jnp.where(row == tm - 1, nxt, pltpu.roll(cu, tm - 1, axis=0))
    cw = cw_ref[...]
    conv = cw[0:1, :] * up + cw[1:2, :] * cu + cw[2:3, :] * dn
    g = (b_ref[...].astype(F32) * conv).astype(BF16)
    o_ref[...] = x_ref[...] + jnp.dot(g, wo_ref[...], preferred_element_type=F32)


def _conv_mix(bcu, x, conv_w, w_out, *, seq, tm=512):
    n, d = x.shape
    hb = BF16_SUBLANES
    tiles_per_halo = tm // hb
    n_halo = n // hb
    prev_idx = lambda i: jnp.maximum(i * tiles_per_halo - 1, 0)
    next_idx = lambda i: jnp.minimum((i + 1) * tiles_per_halo, n_halo - 1)
    return pl.pallas_call(
        functools.partial(_conv_mix_kernel, seq=seq),
        grid=(n // tm,),
        in_specs=[
            pl.BlockSpec((tm, d), lambda i: (i, 0)),
            pl.BlockSpec((tm, d), lambda i: (i, 1)),
            pl.BlockSpec((tm, d), lambda i: (i, 2)),
            pl.BlockSpec((hb, d), lambda i: (prev_idx(i), 1)),
            pl.BlockSpec((hb, d), lambda i: (prev_idx(i), 2)),
            pl.BlockSpec((hb, d), lambda i: (next_idx(i), 1)),
            pl.BlockSpec((hb, d), lambda i: (next_idx(i), 2)),
            pl.BlockSpec((tm, d), lambda i: (i, 0)),
            pl.BlockSpec((3, d), lambda i: (0, 0)),
            pl.BlockSpec((d, d), lambda i: (0, 0)),
        ],
        out_specs=pl.BlockSpec((tm, d), lambda i: (i, 0)),
        out_shape=jax.ShapeDtypeStruct((n, d), F32),
        compiler_params=_params(1),
        name="conv_mix",
    )(bcu, bcu, bcu, bcu, bcu, bcu, bcu, x, conv_w, w_out)


def _mlp_kernel(x_ref, g_ref, w1_ref, w2_ref, gf_ref, o_ref, hn_ref, *, final_norm):
    f = pl.program_id(1)

    @pl.when(f == 0)
    def _():
        _norm_rows_to(hn_ref, x_ref, g_ref, NORM_EPS)
        o_ref[...] = x_ref[...]

    tf = w1_ref.shape[1]
    for c in range(tf // MLP_HIDDEN_GROUP):
        cols = slice(c * MLP_HIDDEN_GROUP, (c + 1) * MLP_HIDDEN_GROUP)
        a = jnp.dot(hn_ref[...], w1_ref[:, cols], preferred_element_type=F32)
        a = jnp.maximum(a, 0.0)
        a = (a * a).astype(BF16)
        o_ref[...] += jnp.dot(a, w2_ref[cols, :], preferred_element_type=F32)

    if final_norm:
        @pl.when(f == pl.num_programs(1) - 1)
        def _():
            _norm_rows_to(o_ref, o_ref, gf_ref, NORM_EPS)


def _mlp(x, g, w1, w2, g_final, *, tm=1024, tf=1024):
    n, d = x.shape
    d_ff = w1.shape[1]
    final_norm = g_final is not None
    gf = (g_final if final_norm else g).reshape(1, d)
    return pl.pallas_call(
        functools.partial(_mlp_kernel, final_norm=final_norm),
        grid=(n // tm, d_ff // tf),
        in_specs=[
            pl.BlockSpec((tm, d), lambda i, f: (i, 0)),
            pl.BlockSpec((1, d), lambda i, f: (0, 0)),
            pl.BlockSpec((d, tf), lambda i, f: (0, f)),
            pl.BlockSpec((tf, d), lambda i, f: (f, 0)),
            pl.BlockSpec((1, d), lambda i, f: (0, 0)),
        ],
        out_specs=pl.BlockSpec((tm, d), lambda i, f: (i, 0)),
        out_shape=jax.ShapeDtypeStruct((n, d), F32),
        scratch_shapes=[pltpu.VMEM((tm, d), BF16)],
        compiler_params=_params(2),
        name="mlp",
    )(x, g.reshape(1, d), w1, w2, gf)


def _attn_tile_ids(t, batch, q_tiles):
    return t // (batch * q_tiles), t % q_tiles, (t // q_tiles) % batch


def _attn_kernel(lq1_ref, lk1_ref, lq2_ref, lk2_ref, subln_ref, q_ref, k_ref, vt_ref,
                 o_ref, rel_ref, t0_ref, t1_ref, m0_ref, m1_ref, acc0_ref, acc1_ref,
                 *, batch, q_tiles, lambda_init):
    n = pl.program_id(0)
    n_tiles = pl.num_programs(0) - ATTN_PIPELINE_DRAIN_STEPS
    h, qi, b = _attn_tile_ids(jnp.minimum(n, n_tiles - 1), batch, q_tiles)
    n_sub, seq, w = t0_ref.shape
    tq = w // 2
    n_kc, _, kc = vt_ref.shape

    @pl.when(n == 0)
    def _():
        t1_ref[...] = jnp.zeros_like(t1_ref)
        m1_ref[...] = jnp.zeros_like(m1_ref)
        acc0_ref[...] = jnp.ones_like(acc0_ref)
        acc1_ref[...] = jnp.ones_like(acc1_ref)

    @pl.when(jnp.logical_and(qi == 0, b == 0))
    def _():
        slope = jnp.exp2(jnp.full((1, 1), -8.0 / N_HEADS, F32) * (h + 1).astype(F32))
        neg_slope = -slope * LOG2E

        def body(c, carry):
            u = c * LANES + lax.broadcasted_iota(jnp.int32, (LANES, LANES), 0)
            col = lax.broadcasted_iota(jnp.int32, (LANES, LANES), 1)
            rows = pl.ds(pl.multiple_of(c * LANES, LANES), LANES)
            rel_ref[rows, :] = neg_slope * jnp.abs(col - u + (seq - LANES)).astype(F32)
            return carry

        lax.fori_loop(0, rel_ref.shape[0] // LANES, body, 0)

    def sub_tile(u, t_cur, m_cur, acc_cur, t_prev, m_prev, acc_prev):
        q_rows = slice(u * tq, (u + 1) * tq)
        bias_base = (seq - LANES) - (qi * n_sub + u) * tq

        def bias_rows(key_start):
            groups = [rel_ref[pl.ds(pl.multiple_of(bias_base + key_start - g * LANES, SUBLANES),
                                    SUBLANES), :] for g in range(tq // LANES)]
            return jnp.concatenate(groups + groups, axis=1)

        acc = acc_prev[u]
        a0, a1 = acc[:HEAD_DIM, :tq], acc[:HEAD_DIM, tq:]
        l0, l1 = acc[HEAD_DIM:HEAD_DIM + 1, :tq], acc[HEAD_DIM:HEAD_DIM + 1, tq:]
        lam = (jnp.exp(jnp.sum(lq1_ref[...] * lk1_ref[...], keepdims=True))
               - jnp.exp(jnp.sum(lq2_ref[...] * lk2_ref[...], keepdims=True)) + lambda_init)
        o_t = a0 / l0 - lam * (a1 / l1)
        ms = jnp.mean(o_t * o_t, axis=0, keepdims=True)
        o_t = o_t * lax.rsqrt(ms + SUBLN_EPS) * subln_ref[...] * (1.0 - lambda_init)
        o_ref[q_rows, :] = o_t.T.astype(o_ref.dtype)

        q = q_ref[q_rows, :]
        lane = lax.broadcasted_iota(jnp.int32, q.shape, 1)
        zero = jnp.zeros_like(q)
        q_both = jnp.concatenate([jnp.where(lane < HALF_DIM, q, zero),
                                  jnp.where(lane >= HALF_DIM, q, zero)], axis=0)
        m_run = jnp.full((SUBLANES, w), -1e30, F32)

        def pass1_chunk(j, m_run):
            s = lax.dot_general(k_ref[j * kc:(j + 1) * kc, :], q_both, NT_DIMS,
                                preferred_element_type=F32)
            for r in range(kc // SUBLANES):
                rows = slice(j * kc + r * SUBLANES, j * kc + (r + 1) * SUBLANES)
                t = s[r * SUBLANES:(r + 1) * SUBLANES, :] + bias_rows(rows.start)
                t_cur[u, rows, :] = t
                m_run = jnp.maximum(m_run, t)
            return m_run

        ones_rows = (lax.broadcasted_iota(jnp.int32, (BF16_SUBLANES, kc), 0) == 0).astype(BF16)
        m = m_prev[u]
        slab = BF16_SUBLANES

        def pass2_chunk(j):
            parts = []
            for r in range(kc // slab):
                rows = slice(j * kc + r * slab, j * kc + (r + 1) * slab)
                x = t_prev[u, rows, :].reshape(slab // SUBLANES, SUBLANES, w) - m[None]
                parts.append(jnp.exp2(x.reshape(slab, w).astype(BF16)))
            e = jnp.concatenate(parts, axis=0)
            v_aug = jnp.concatenate([vt_ref[j], ones_rows], axis=0)
            return jnp.dot(v_aug, e, preferred_element_type=F32)

        acc = None
        for j in range(n_kc):
            m_run = pass1_chunk(j, m_run)
            pv = pass2_chunk(j)
            acc = pv if acc is None else acc + pv
        m_cur[u] = jnp.broadcast_to(jnp.max(m_run, axis=0, keepdims=True), (SUBLANES, w))
        acc_cur[u] = acc

    def step(*bufs):
        for u in range(n_sub):
            sub_tile(u, *bufs)

    @pl.when(n % 2 == 0)
    def _():
        step(t0_ref, m0_ref, acc0_ref, t1_ref, m1_ref, acc1_ref)

    @pl.when(n % 2 == 1)
    def _():
        step(t1_ref, m1_ref, acc1_ref, t0_ref, m0_ref, acc0_ref)


def _attention(qk, vt, lq1, lk1, lq2, lk2, subln, *, batch, seq, lambda_init, tq=256, n_sub=2):
    n = qk.shape[0]
    d = N_HEADS * HEAD_DIM
    kc = vt.shape[2]
    chunks = seq // kc
    rows = n_sub * tq
    q_tiles = seq // rows
    n_tiles = N_HEADS * q_tiles * batch
    acc_rows = HEAD_DIM + BF16_SUBLANES
    tile = lambda s, lag: _attn_tile_ids(jnp.clip(s - lag, 0, n_tiles - 1), batch, q_tiles)

    def q_map(s):
        h, qi, b = tile(s, 0)
        return b * q_tiles + qi, h

    def k_map(s):
        h, _, b = tile(s, 0)
        return b, N_HEADS + h

    def vt_map(s):
        h, _, b = tile(s, 1)
        return b, h, 0

    def o_map(s):
        h, qi, b = tile(s, 2)
        return b * q_tiles + qi, h

    vec = lambda a: a.reshape(1, -1).astype(F32)
    small = lambda width: pl.BlockSpec((1, width), lambda s: (0, 0))
    return pl.pallas_call(
        functools.partial(_attn_kernel, batch=batch, q_tiles=q_tiles, lambda_init=lambda_init),
        grid=(n_tiles + ATTN_PIPELINE_DRAIN_STEPS,),
        in_specs=[
            small(HALF_DIM), small(HALF_DIM), small(HALF_DIM), small(HALF_DIM),
            pl.BlockSpec((HEAD_DIM, 1), lambda s: (0, 0)),
            pl.BlockSpec((rows, HEAD_DIM), q_map),
            pl.BlockSpec((seq, HEAD_DIM), k_map),
            pl.BlockSpec((chunks, HEAD_DIM, kc), vt_map),
        ],
        out_specs=pl.BlockSpec((rows, HEAD_DIM), o_map),
        out_shape=jax.ShapeDtypeStruct((n, d), BF16),
        scratch_shapes=[
            pltpu.VMEM((2 * seq - LANES, LANES), F32),
            pltpu.VMEM((n_sub, seq, 2 * tq), F32),
            pltpu.VMEM((n_sub, seq, 2 * tq), F32),
            pltpu.VMEM((n_sub, SUBLANES, 2 * tq), F32),
            pltpu.VMEM((n_sub, SUBLANES, 2 * tq), F32),
            pltpu.VMEM((n_sub, acc_rows, 2 * tq), F32),
            pltpu.VMEM((n_sub, acc_rows, 2 * tq), F32),
        ],
        compiler_params=_params(1),
        name="diff_attention",
    )(vec(lq1), vec(lk1), vec(lq2), vec(lk2), subln.reshape(HEAD_DIM, 1).astype(F32), qk, qk, vt)


def _proj_residual_kernel(a_ref, x_ref, w_ref, o_ref):
    o_ref[...] = x_ref[...] + jnp.dot(a_ref[...], w_ref[...], preferred_element_type=F32)


def _proj_residual(a, x, w, *, tm=512):
    n, d = x.shape
    return pl.pallas_call(
        _proj_residual_kernel,
        grid=(n // tm,),
        in_specs=[
            pl.BlockSpec((tm, d), lambda i: (i, 0)),
            pl.BlockSpec((tm, d), lambda i: (i, 0)),
            pl.BlockSpec((d, d), lambda i: (0, 0)),
        ],
        out_specs=pl.BlockSpec((tm, d), lambda i: (i, 0)),
        out_shape=jax.ShapeDtypeStruct((n, d), F32),
        compiler_params=_params(1),
        name="proj_residual",
    )(a, x, w)


def kernel(x, ln_mix, ln_mlp, conv_w_in, conv_w, conv_w_out, attn_w_qkv, attn_lambda_q1,
           attn_lambda_k1, attn_lambda_q2, attn_lambda_k2, attn_subln, attn_w_o, mlp_w1,
           mlp_w2, ln_f):
    batch, seq, d = x.shape
    depth = ln_mix.shape[0]
    h = x.reshape(batch * seq, d)
    for i in range(depth):
        j = i // N_MIXERS
        if i % N_MIXERS == 0:
            bcu = _norm_matmul(h, ln_mix[i], _weight_bf16(conv_w_in, j))
            h = _conv_mix(bcu, h, conv_w[j], _weight_bf16(conv_w_out, j), seq=seq)
        else:
            qk, hn = _norm_matmul(h, ln_mix[i], _weight_bf16(attn_w_qkv, j, cols=2 * d),
                                  emit_normed=True, scaled_cols=d, scale=ATTN_SCALE * LOG2E)
            vt = _matmul_t(hn, attn_w_qkv[j][:, 2 * d:].T.astype(BF16), chunk=ATTN_KEY_CHUNK)
            o = _attention(qk, vt, attn_lambda_q1[j], attn_lambda_k1[j], attn_lambda_q2[j],
                           attn_lambda_k2[j], attn_subln[j], batch=batch, seq=seq,
                           lambda_init=_lambda_init(i))
            h = _proj_residual(o, h, _weight_bf16(attn_w_o, j))
        g_final = ln_f if i == depth - 1 else None
        h = _mlp(h, ln_mlp[i], _weight_bf16(mlp_w1, i), _weight_bf16(mlp_w2, i), g_final)
    return h.reshape(batch, seq, d)
```

```python
import functools
import math

import jax
import jax.numpy as jnp
from jax import lax
from jax.experimental import pallas as pl
from jax.experimental.pallas import tpu as pltpu

N_HEADS = 16
HEAD_DIM = 128
HALF_DIM = HEAD_DIM // 2
ATTN_SCALE = HALF_DIM ** -0.5
LOG2E = math.log2(math.e)
NORM_EPS = 1e-6
SUBLN_EPS = 1e-5
N_MIXERS = 2

V7X_VMEM_BYTES = 64 * 1024 * 1024
VMEM_LIMIT_BYTES = V7X_VMEM_BYTES - 8 * 1024 * 1024
LANES = 128
SUBLANES = 8
BF16_SUBLANES = 16
NORM_ROWS = 256
MLP_HIDDEN_GROUP = 512
CAST_BLOCK_ELEMS = 2 * 1024 * 1024
ATTN_KEY_CHUNK = 512
ATTN_PIPELINE_DRAIN_STEPS = 2

F32 = jnp.float32
BF16 = jnp.bfloat16
NT_DIMS = (((1,), (1,)), ((), ()))


def _lambda_init(layer_idx):
    return 0.8 - 0.6 * math.exp(-0.3 * layer_idx)


def _rmsnorm(x, g, eps):
    ms = jnp.mean(x * x, axis=-1, keepdims=True)
    return x * lax.rsqrt(ms + eps) * g


def _norm_rows_to(dst_ref, src_ref, g_ref, eps):
    rows = src_ref.shape[0]
    g = g_ref[...]

    def body(c, carry):
        sl = pl.ds(pl.multiple_of(c * NORM_ROWS, NORM_ROWS), NORM_ROWS)
        dst_ref[sl, :] = _rmsnorm(src_ref[sl, :], g, eps).astype(dst_ref.dtype)
        return carry

    lax.fori_loop(0, rows // NORM_ROWS, body, 0)


def _params(n_axes):
    return pltpu.CompilerParams(
        dimension_semantics=("arbitrary",) * n_axes,
        vmem_limit_bytes=VMEM_LIMIT_BYTES)


def _cast_kernel(w_ref, o_ref):
    o_ref[...] = w_ref[...].astype(o_ref.dtype)


def _weight_bf16(w, layer, *, cols=None):
    _, r, c = w.shape
    c = c if cols is None else cols
    tr = min(r, 1 << ((CAST_BLOCK_ELEMS // c).bit_length() - 1))
    return pl.pallas_call(
        _cast_kernel,
        grid=(r // tr,),
        in_specs=[pl.BlockSpec((None, tr, c), lambda i: (layer, i, 0))],
        out_specs=pl.BlockSpec((tr, c), lambda i: (i, 0)),
        out_shape=jax.ShapeDtypeStruct((r, c), BF16),
        compiler_params=_params(1),
        name="weight_bf16",
    )(w)


def _norm_matmul_kernel(x_ref, g_ref, w_ref, o_ref, hn_ref, *, scaled_tiles, scale):
    j = pl.program_id(1)

    @pl.when(j == 0)
    def _():
        _norm_rows_to(hn_ref, x_ref, g_ref, NORM_EPS)

    acc = jnp.dot(hn_ref[...], w_ref[...], preferred_element_type=F32)
    if scaled_tiles:
        acc = acc * jnp.where(j < scaled_tiles, scale, 1.0).astype(F32)
    o_ref[...] = acc.astype(o_ref.dtype)


def _norm_matmul(x, g, w, *, emit_normed=False, scaled_cols=0, scale=1.0, tm=1024, tn=2048):
    n, d = x.shape
    n_out = w.shape[1]
    out_specs = [pl.BlockSpec((tm, tn), lambda i, j: (i, j))]
    out_shape = [jax.ShapeDtypeStruct((n, n_out), BF16)]
    scratch_shapes = []
    if emit_normed:
        out_specs.append(pl.BlockSpec((tm, d), lambda i, j: (i, 0)))
        out_shape.append(jax.ShapeDtypeStruct((n, d), BF16))
    else:
        scratch_shapes.append(pltpu.VMEM((tm, d), BF16))
    outs = pl.pallas_call(
        functools.partial(_norm_matmul_kernel, scaled_tiles=scaled_cols // tn, scale=scale),
        grid=(n // tm, n_out // tn),
        in_specs=[
            pl.BlockSpec((tm, d), lambda i, j: (i, 0)),
            pl.BlockSpec((1, d), lambda i, j: (0, 0)),
            pl.BlockSpec((d, tn), lambda i, j: (0, j)),
        ],
        out_specs=out_specs,
        out_shape=out_shape,
        scratch_shapes=scratch_shapes,
        compiler_params=_params(2),
        name="norm_matmul",
    )(x, g.reshape(1, d), w)
    return outs if emit_normed else outs[0]


def _matmul_t_kernel(a_ref, wt_ref, o_ref):
    chunk = o_ref.shape[2]
    out = lax.dot_general(wt_ref[...], a_ref[...], NT_DIMS,
                          preferred_element_type=F32).astype(o_ref.dtype)
    for c in range(o_ref.shape[0]):
        o_ref[c] = out[:, c * chunk:(c + 1) * chunk]


def _matmul_t(a, wt, *, chunk, tm=1024, tn=1024):
    n, d = a.shape
    n_out = wt.shape[0]
    return pl.pallas_call(
        _matmul_t_kernel,
        grid=(n // tm, n_out // tn),
        in_specs=[
            pl.BlockSpec((tm, d), lambda i, j: (i, 0)),
            pl.BlockSpec((tn, d), lambda i, j: (j, 0)),
        ],
        out_specs=pl.BlockSpec((tm // chunk, tn, chunk), lambda i, j: (i, j, 0)),
        out_shape=jax.ShapeDtypeStruct((n // chunk, n_out, chunk), BF16),
        compiler_params=_params(2),
        name="matmul_t",
    )(a, wt)


def _conv_mix_kernel(b_ref, c_ref, u_ref, cp_ref, up_ref, cn_ref, un_ref,
                     x_ref, cw_ref, wo_ref, o_ref, *, seq):
    i = pl.program_id(0)
    tm = b_ref.shape[0]
    cu = c_ref[...].astype(F32) * u_ref[...].astype(F32)
    last = BF16_SUBLANES - 1
    prev = cp_ref[last:, :].astype(F32) * up_ref[last:, :].astype(F32)
    nxt = cn_ref[:1, :].astype(F32) * un_ref[:1, :].astype(F32)
    prev = jnp.where((i * tm) % seq == 0, 0.0, prev)
    nxt = jnp.where(((i + 1) * tm) % seq == 0, 0.0, nxt)
    row = lax.broadcasted_iota(jnp.int32, cu.shape, 0)
    up = jnp.where(row == 0, prev, pltpu.roll(cu, 1, axis=0))
    dn = jnp.where(row == tm - 1, nxt, pltpu.roll(cu, tm - 1, axis=0))
    cw = cw_ref[...]
    conv = cw[0:1, :] * up + cw[1:2, :] * cu + cw[2:3, :] * dn
    g = (b_ref[...].astype(F32) * conv).astype(BF16)
    o_ref[...] = x_ref[...] + jnp.dot(g, wo_ref[...], preferred_element_type=F32)


def _conv_mix(bcu, x, conv_w, w_out, *, seq, tm=512):
    n, d = x.shape
    hb = BF16_SUBLANES
    tiles_per_halo = tm // hb
    n_halo = n // hb
    prev_idx = lambda i: jnp.maximum(i * tiles_per_halo - 1, 0)
    next_idx = lambda i: jnp.minimum((i + 1) * tiles_per_halo, n_halo - 1)
    return pl.pallas_call(
        functools.partial(_conv_mix_kernel, seq=seq),
        grid=(n // tm,),
        in_specs=[
            pl.BlockSpec((tm, d), lambda i: (i, 0)),
            pl.BlockSpec((tm, d), lambda i: (i, 1)),
            pl.BlockSpec((tm, d), lambda i: (i, 2)),
            pl.BlockSpec((hb, d), lambda i: (prev_idx(i), 1)),
            pl.BlockSpec((hb, d), lambda i: (prev_idx(i), 2)),
            pl.BlockSpec((hb, d), lambda i: (next_idx(i), 1)),
            pl.BlockSpec((hb, d), lambda i: (next_idx(i), 2)),
            pl.BlockSpec((tm, d), lambda i: (i, 0)),
            pl.BlockSpec((3, d), lambda i: (0, 0)),
            pl.BlockSpec((d, d), lambda i: (0, 0)),
        ],
        out_specs=pl.BlockSpec((tm, d), lambda i: (i, 0)),
        out_shape=jax.ShapeDtypeStruct((n, d), F32),
        compiler_params=_params(1),
        name="conv_mix",
    )(bcu, bcu, bcu, bcu, bcu, bcu, bcu, x, conv_w, w_out)


def _mlp_kernel(x_ref, g_ref, w1_ref, w2_ref, gf_ref, o_ref, hn_ref, *, final_norm):
    f = pl.program_id(1)

    @pl.when(f == 0)
    def _():
        _norm_rows_to(hn_ref, x_ref, g_ref, NORM_EPS)
        o_ref[...] = x_ref[...]

    tf = w1_ref.shape[1]
    for c in range(tf // MLP_HIDDEN_GROUP):
        cols = slice(c * MLP_HIDDEN_GROUP, (c + 1) * MLP_HIDDEN_GROUP)
        a = jnp.dot(hn_ref[...], w1_ref[:, cols], preferred_element_type=F32)
        a = jnp.maximum(a, 0.0)
        a = (a * a).astype(BF16)
        o_ref[...] += jnp.dot(a, w2_ref[cols, :], preferred_element_type=F32)

    if final_norm:
        @pl.when(f == pl.num_programs(1) - 1)
        def _():
            _norm_rows_to(o_ref, o_ref, gf_ref, NORM_EPS)


def _mlp(x, g, w1, w2, g_final, *, tm=1024, tf=1024):
    n, d = x.shape
    d_ff = w1.shape[1]
    final_norm = g_final is not None
    gf = (g_final if final_norm else g).reshape(1, d)
    return pl.pallas_call(
        functools.partial(_mlp_kernel, final_norm=final_norm),
        grid=(n // tm, d_ff // tf),
        in_specs=[
            pl.BlockSpec((tm, d), lambda i, f: (i, 0)),
            pl.BlockSpec((1, d), lambda i, f: (0, 0)),
            pl.BlockSpec((d, tf), lambda i, f: (0, f)),
            pl.BlockSpec((tf, d), lambda i, f: (f, 0)),
            pl.BlockSpec((1, d), lambda i, f: (0, 0)),
        ],
        out_specs=pl.BlockSpec((tm, d), lambda i, f: (i, 0)),
        out_shape=jax.ShapeDtypeStruct((n, d), F32),
        scratch_shapes=[pltpu.VMEM((tm, d), BF16)],
        compiler_params=_params(2),
        name="mlp",
    )(x, g.reshape(1, d), w1, w2, gf)


def _attn_tile_ids(t, batch, q_tiles):
    return t // (batch * q_tiles), t % q_tiles, (t // q_tiles) % batch


def _attn_kernel(lq1_ref, lk1_ref, lq2_ref, lk2_ref, subln_ref, q_ref, k_ref, vt_ref,
                 o_ref, rel_ref, t_ref, m_ref, acc_ref, *, batch, q_tiles, lambda_init):
    n = pl.program_id(0)
    n_tiles = pl.num_programs(0) - ATTN_PIPELINE_DRAIN_STEPS
    h, qi, b = _attn_tile_ids(jnp.minimum(n, n_tiles - 1), batch, q_tiles)
    n_sub, seq, w = t_ref.shape
    tq = w // 2
    n_kc, _, kc = vt_ref.shape

    @pl.when(n == 0)
    def _():
        t_ref[...] = jnp.zeros_like(t_ref)
        m_ref[...] = jnp.zeros_like(m_ref)
        acc_ref[...] = jnp.ones_like(acc_ref)

    @pl.when(jnp.logical_and(qi == 0, b == 0))
    def _():
        slope = jnp.exp2(jnp.full((1, 1), -8.0 / N_HEADS, F32) * (h + 1).astype(F32))
        neg_slope = -slope * LOG2E

        def body(c, carry):
            u = c * LANES + lax.broadcasted_iota(jnp.int32, (LANES, LANES), 0)
            col = lax.broadcasted_iota(jnp.int32, (LANES, LANES), 1)
            rows = pl.ds(pl.multiple_of(c * LANES, LANES), LANES)
            rel_ref[rows, :] = neg_slope * jnp.abs(col - u + (seq - LANES)).astype(F32)
            return carry

        lax.fori_loop(0, rel_ref.shape[0] // LANES, body, 0)

    def sub_tile(u):
        q_rows = slice(u * tq, (u + 1) * tq)
        bias_base = (seq - LANES) - (qi * n_sub + u) * tq

        def bias_rows(key_start):
            groups = [rel_ref[pl.ds(pl.multiple_of(bias_base + key_start - g * LANES, SUBLANES),
                                    SUBLANES), :] for g in range(tq // LANES)]
            return jnp.concatenate(groups + groups, axis=1)

        acc = acc_ref[u]
        a0, a1 = acc[:HEAD_DIM, :tq], acc[:HEAD_DIM, tq:]
        l0, l1 = acc[HEAD_DIM:HEAD_DIM + 1, :tq], acc[HEAD_DIM:HEAD_DIM + 1, tq:]
        lam = (jnp.exp(jnp.sum(lq1_ref[...] * lk1_ref[...], keepdims=True))
               - jnp.exp(jnp.sum(lq2_ref[...] * lk2_ref[...], keepdims=True)) + lambda_init)
        o_t = a0 / l0 - lam * (a1 / l1)
        ms = jnp.mean(o_t * o_t, axis=0, keepdims=True)
        o_t = o_t * lax.rsqrt(ms + SUBLN_EPS) * subln_ref[...] * (1.0 - lambda_init)
        o_ref[q_rows, :] = o_t.T.astype(o_ref.dtype)

        q = q_ref[q_rows, :]
        lane = lax.broadcasted_iota(jnp.int32, q.shape, 1)
        zero = jnp.zeros_like(q)
        q_both = jnp.concatenate([jnp.where(lane < HALF_DIM, q, zero),
                                  jnp.where(lane >= HALF_DIM, q, zero)], axis=0)
        m_run = jnp.full((SUBLANES, w), -1e30, F32)

        def pass1_chunk(j, m_run):
            s = lax.dot_general(k_ref[j * kc:(j + 1) * kc, :], q_both, NT_DIMS,
                                preferred_element_type=F32)
            for r in range(kc // SUBLANES):
                rows = slice(j * kc + r * SUBLANES, j * kc + (r + 1) * SUBLANES)
                t = s[r * SUBLANES:(r + 1) * SUBLANES, :] + bias_rows(rows.start)
                t_ref[u, rows, :] = t
                m_run = jnp.maximum(m_run, t)
            return m_run

        ones_rows = (lax.broadcasted_iota(jnp.int32, (BF16_SUBLANES, kc), 0) == 0).astype(BF16)
        m = m_ref[u]
        slab = BF16_SUBLANES

        def pass2_chunk(j):
            parts = []
            for r in range(kc // slab):
                rows = slice(j * kc + r * slab, j * kc + (r + 1) * slab)
                x = t_ref[u, rows, :].reshape(slab // SUBLANES, SUBLANES, w) - m[None]
                parts.append(jnp.exp2(x.reshape(slab, w).astype(BF16)))
            e = jnp.concatenate(parts, axis=0)
            v_aug = jnp.concatenate([vt_ref[j], ones_rows], axis=0)
            return jnp.dot(v_aug, e, preferred_element_type=F32)

        acc = None
        for j in range(n_kc):
            pv = pass2_chunk(j)
            m_run = pass1_chunk(j, m_run)
            acc = pv if acc is None else acc + pv
        m_ref[u] = jnp.broadcast_to(jnp.max(m_run, axis=0, keepdims=True), (SUBLANES, w))
        acc_ref[u] = acc

    for u in range(n_sub):
        sub_tile(u)


def _attention(qk, vt, lq1, lk1, lq2, lk2, subln, *, batch, seq, lambda_init, tq=256, n_sub=4):
    n = qk.shape[0]
    d = N_HEADS * HEAD_DIM
    kc = vt.shape[2]
    chunks = seq // kc
    rows = n_sub * tq
    q_tiles = seq // rows
    n_tiles = N_HEADS * q_tiles * batch
    acc_rows = HEAD_DIM + BF16_SUBLANES
    tile = lambda s, lag: _attn_tile_ids(jnp.clip(s - lag, 0, n_tiles - 1), batch, q_tiles)

    def q_map(s):
        h, qi, b = tile(s, 0)
        return b * q_tiles + qi, h

    def k_map(s):
        h, _, b = tile(s, 0)
        return b, N_HEADS + h

    def vt_map(s):
        h, _, b = tile(s, 1)
        return b, h, 0

    def o_map(s):
        h, qi, b = tile(s, 2)
        return b * q_tiles + qi, h

    vec = lambda a: a.reshape(1, -1).astype(F32)
    small = lambda width: pl.BlockSpec((1, width), lambda s: (0, 0))
    return pl.pallas_call(
        functools.partial(_attn_kernel, batch=batch, q_tiles=q_tiles, lambda_init=lambda_init),
        grid=(n_tiles + ATTN_PIPELINE_DRAIN_STEPS,),
        in_specs=[
            small(HALF_DIM), small(HALF_DIM), small(HALF_DIM), small(HALF_DIM),
            pl.BlockSpec((HEAD_DIM, 1), lambda s: (0, 0)),
            pl.BlockSpec((rows, HEAD_DIM), q_map),
            pl.BlockSpec((seq, HEAD_DIM), k_map),
            pl.BlockSpec((chunks, HEAD_DIM, kc), vt_map),
        ],
        out_specs=pl.BlockSpec((rows, HEAD_DIM), o_map),
        out_shape=jax.ShapeDtypeStruct((n, d), BF16),
        scratch_shapes=[
            pltpu.VMEM((2 * seq - LANES, LANES), F32),
            pltpu.VMEM((n_sub, seq, 2 * tq), F32),
            pltpu.VMEM((n_sub, SUBLANES, 2 * tq), F32),
            pltpu.VMEM((n_sub, acc_rows, 2 * tq), F32),
        ],
        compiler_params=_params(1),
        name="diff_attention",
    )(vec(lq1), vec(lk1), vec(lq2), vec(lk2), subln.reshape(HEAD_DIM, 1).astype(F32), qk, qk, vt)


def _proj_residual_kernel(a_ref, x_ref, w_ref, o_ref):
    o_ref[...] = x_ref[...] + jnp.dot(a_ref[...], w_ref[...], preferred_element_type=F32)


def _proj_residual(a, x, w, *, tm=512):
    n, d = x.shape
    return pl.pallas_call(
        _proj_residual_kernel,
        grid=(n // tm,),
        in_specs=[
            pl.BlockSpec((tm, d), lambda i: (i, 0)),
            pl.BlockSpec((tm, d), lambda i: (i, 0)),
            pl.BlockSpec((d, d), lambda i: (0, 0)),
        ],
        out_specs=pl.BlockSpec((tm, d), lambda i: (i, 0)),
        out_shape=jax.ShapeDtypeStruct((n, d), F32),
        compiler_params=_params(1),
        name="proj_residual",
    )(a, x, w)


def kernel(x, ln_mix, ln_mlp, conv_w_in, conv_w, conv_w_out, attn_w_qkv, attn_lambda_q1,
           attn_lambda_k1, attn_lambda_q2, attn_lambda_k2, attn_subln, attn_w_o, mlp_w1,
           mlp_w2, ln_f):
    batch, seq, d = x.shape
    depth = ln_mix.shape[0]
    h = x.reshape(batch * seq, d)
    for i in range(depth):
        j = i // N_MIXERS
        if i % N_MIXERS == 0:
            bcu = _norm_matmul(h, ln_mix[i], _weight_bf16(conv_w_in, j))
            h = _conv_mix(bcu, h, conv_w[j], _weight_bf16(conv_w_out, j), seq=seq)
        else:
            qk, hn = _norm_matmul(h, ln_mix[i], _weight_bf16(attn_w_qkv, j, cols=2 * d),
                                  emit_normed=True, scaled_cols=d, scale=ATTN_SCALE * LOG2E)
            vt = _matmul_t(hn, attn_w_qkv[j][:, 2 * d:].T.astype(BF16), chunk=ATTN_KEY_CHUNK)
            o = _attention(qk, vt, attn_lambda_q1[j], attn_lambda_k1[j], attn_lambda_q2[j],
                           attn_lambda_k2[j], attn_subln[j], batch=batch, seq=seq,
                           lambda_init=_lambda_init(i))
            h = _proj_residual(o, h, _weight_bf16(attn_w_o, j))
        g_final = ln_f if i == depth - 1 else None
        h = _mlp(h, ln_mlp[i], _weight_bf16(mlp_w1, i), _weight_bf16(mlp_w2, i), g_final)
    return h.reshape(batch, seq, d)
```
